```python
import jax, jax.numpy as jnp
from jax import lax
import numpy as np

D_MODEL = 2048
BATCH = 2
SEQ = 8192
DEPTH = 4

CHUNK = 64
EPS = 1e-6
POOL_WIDTH = D_MODEL // 2
POOL_GROUPS = 4
POOL_GROUP_DIM = POOL_WIDTH // POOL_GROUPS
POOL_WINDOWS = (2, 4, 8, 16)
RET_HEADS = 4
RET_HEAD_DIM = D_MODEL // 2 // RET_HEADS
RET_WIDTH = RET_HEADS * RET_HEAD_DIM
EVEN_IN_WIDTH = POOL_WIDTH + 4 * RET_WIDTH
ROPE_BASE = 10000.0
CONV_WIDTH = 3
CONV_DIM = D_MODEL
FFN_DIM = 5632
N_EXPERTS = 8
TOP_K = 2
EXPERT_DIM = FFN_DIM // TOP_K
N_EVEN = (DEPTH + 1) // 2
N_ODD = DEPTH // 2

kernel_name = "hybrid_pool_retention_shortconv_moe"


def rmsnorm(x, g):
    xf = x.astype(jnp.float32)
    y = xf * lax.rsqrt(jnp.mean(xf * xf, axis=-1, keepdims=True) + EPS)
    return (y * g.astype(jnp.float32)).astype(x.dtype)


def pool_mixer(u, w_pool, scale):
    b, s, _ = u.shape
    uf = u.astype(jnp.float32)
    cs = jnp.pad(jnp.cumsum(uf, axis=1), ((0, 0), (1, 0), (0, 0)))
    t = jnp.arange(s)
    outs = []
    for g, w in enumerate(POOL_WINDOWS):
        c = cs[..., g * POOL_GROUP_DIM:(g + 1) * POOL_GROUP_DIM]
        c_lo = jnp.pad(c, ((0, 0), (w - 1, 0), (0, 0)))[:, :s]
        count = jnp.minimum(t + 1, w).astype(jnp.float32)[None, :, None]
        mean = (c[:, 1:] - c_lo) / count
        outs.append(mean - uf[..., g * POOL_GROUP_DIM:(g + 1) * POOL_GROUP_DIM])
    p = jnp.stack(outs, axis=2).astype(u.dtype)
    y = jnp.einsum('bsgc,gcd->bsgd', p, w_pool).reshape(b, s, POOL_WIDTH)
    return y * scale


def rope(x, pos):
    half = x.shape[-1] // 2
    inv = ROPE_BASE ** (-jnp.arange(half, dtype=jnp.float32) / half)
    ang = pos.astype(jnp.float32)[:, None] * inv[None, :]
    cos = jnp.cos(ang)[None, :, None, :]
    sin = jnp.sin(ang)[None, :, None, :]
    x1, x2 = x[..., :half], x[..., half:]
    return jnp.concatenate([x1 * cos - x2 * sin, x2 * cos + x1 * sin], axis=-1)


def retention(q, k, v):
    b, s, h, dk = q.shape
    dv = v.shape[-1]
    nc = s // CHUNK
    log_g = jnp.log(1.0 - 2.0 ** (-5.0 - jnp.arange(h, dtype=jnp.float32)))
    idx = jnp.arange(CHUNK, dtype=jnp.float32)
    intra = jnp.exp(log_g[:, None, None] * jnp.abs(idx[:, None] - idx[None, :]))
    q_decay = jnp.exp(log_g[:, None] * (idx + 1.0))[None, :, :, None]
    k_decay = jnp.exp(log_g[:, None] * (CHUNK - 1.0 - idx))[None, :, :, None]
    chunk_decay = jnp.exp(log_g * CHUNK)[None, :, None, None]

    def to_chunks(a):
        return a.reshape(b, nc, CHUNK, h, a.shape[-1]).transpose(1, 0, 3, 2, 4)

    qc = to_chunks(q) * (dk ** -0.5)
    kc = to_chunks(k)
    vc = to_chunks(v)

    def step(state, inp):
        qi, ki, vi = inp
        scores = jnp.einsum('bhid,bhjd->bhij', qi, ki) * intra
        o = (jnp.einsum('bhij,bhjd->bhid', scores, vi)
             + jnp.einsum('bhid,bhde->bhie', qi * q_decay, state))
        state = state * chunk_decay + jnp.einsum('bhjd,bhje->bhde', ki * k_decay, vi)
        return state, o

    state0 = jnp.zeros((b, h, dk, dv), jnp.float32)
    _, o = lax.scan(step, state0, (qc, kc, vc))
    return o.transpose(1, 0, 3, 2, 4).reshape(b, s, h, dv)


def head_groupnorm(o):
    mu = jnp.mean(o, axis=-1, keepdims=True)
    var = jnp.mean(jnp.square(o - mu), axis=-1, keepdims=True)
    return (o - mu) * lax.rsqrt(var + 1e-5)


def even_mixer(xn, w_in, w_pool, pool_scale, w_out, pos):
    b, s, _ = xn.shape
    proj = xn @ w_in
    u, q, k, v, g = jnp.split(
        proj, [POOL_WIDTH, POOL_WIDTH + RET_WIDTH, POOL_WIDTH + 2 * RET_WIDTH,
               POOL_WIDTH + 3 * RET_WIDTH], axis=-1)
    y_pool = pool_mixer(u, w_pool, pool_scale)
    heads = lambda a: a.astype(jnp.float32).reshape(b, s, RET_HEADS, RET_HEAD_DIM)
    qh = rope(heads(q), pos)
    kh = rope(heads(k), pos)
    o = head_groupnorm(retention(qh, kh, heads(v))).reshape(b, s, RET_WIDTH)
    y_ret = (o * jax.nn.silu(g.astype(jnp.float32))).astype(xn.dtype)
    return jnp.concatenate([y_pool, y_ret], axis=-1) @ w_out


def short_conv_mixer(xn, w_in, conv_w, w_out):
    s = xn.shape[1]
    bg, cg, hx = jnp.split(xn @ w_in, 3, axis=-1)
    z = cg * hx
    zp = jnp.pad(z, ((0, 0), (CONV_WIDTH - 1, 0), (0, 0)))
    conv = zp[:, 0:s] * conv_w[0]
    for i in range(1, CONV_WIDTH):
        conv = conv + zp[:, i:i + s] * conv_w[i]
    return (bg * conv) @ w_out


def swiglu(x, w_gu, w_down):
    gate, up = jnp.split(x @ w_gu, 2, axis=-1)
    return (jax.nn.silu(gate) * up) @ w_down


def moe_swiglu(xn, router, w_gu, w_down):
    b, s, d = xn.shape
    t = xn.reshape(b * s, d)
    logits = (t @ router).astype(jnp.float32)
    top_val, top_idx = lax.top_k(logits, TOP_K)
    weights = jax.nn.softmax(top_val, axis=-1)
    gate = jnp.sum(jax.nn.one_hot(top_idx, N_EXPERTS, dtype=jnp.float32)
                   * weights[..., None], axis=1).astype(t.dtype)
    y = jnp.zeros_like(t)
    for e in range(N_EXPERTS):
        y = y + gate[:, e:e + 1] * swiglu(t, w_gu[e], w_down[e])
    return y.reshape(b, s, d)


def setup_inputs(seed: int = 0) -> dict:
    key = jax.random.key(seed)
    ks = jax.random.split(key, 20)
    f32 = jnp.float32

    def w(k, shape, fan_in):
        return jax.random.normal(k, shape, f32) * (fan_in ** -0.5)

    def gain(k, shape):
        return 1.0 + 0.02 * jax.random.normal(k, shape, f32)

    return {
        "x": jax.random.normal(ks[0], (BATCH, SEQ, D_MODEL), f32),
        "norm_mix": gain(ks[1], (DEPTH, D_MODEL)),
        "norm_ffn": gain(ks[2], (DEPTH, D_MODEL)),
        "norm_final": gain(ks[3], (D_MODEL,)),
        "ev_w_in": w(ks[4], (N_EVEN, D_MODEL, EVEN_IN_WIDTH), D_MODEL),
        "ev_pool_w": w(ks[5], (N_EVEN, POOL_GROUPS, POOL_GROUP_DIM, POOL_GROUP_DIM), POOL_GROUP_DIM),
        "ev_pool_scale": gain(ks[6], (N_EVEN, POOL_WIDTH)),
        "ev_w_out": w(ks[7], (N_EVEN, POOL_WIDTH + RET_WIDTH, D_MODEL), POOL_WIDTH + RET_WIDTH),
        "od_w_in": w(ks[8], (N_ODD, D_MODEL, 3 * CONV_DIM), D_MODEL),
        "od_conv_w": w(ks[9], (N_ODD, CONV_WIDTH, CONV_DIM), CONV_WIDTH),
        "od_w_out": w(ks[10], (N_ODD, CONV_DIM, D_MODEL), CONV_DIM),
        "ffn_w_gu": w(ks[11], (N_EVEN, D_MODEL, 2 * FFN_DIM), D_MODEL),
        "ffn_w_down": w(ks[12], (N_EVEN, FFN_DIM, D_MODEL), FFN_DIM),
        "moe_router": w(ks[13], (N_ODD, D_MODEL, N_EXPERTS), D_MODEL),
        "moe_w_gu": w(ks[14], (N_ODD, N_EXPERTS, D_MODEL, 2 * EXPERT_DIM), D_MODEL),
        "moe_w_down": w(ks[15], (N_ODD, N_EXPERTS, EXPERT_DIM, D_MODEL), EXPERT_DIM),
    }


def reference(x, norm_mix, norm_ffn, norm_final, ev_w_in, ev_pool_w, ev_pool_scale, ev_w_out,
              od_w_in, od_conv_w, od_w_out, ffn_w_gu, ffn_w_down, moe_router, moe_w_gu,
              moe_w_down):
    pos = jnp.arange(x.shape[1])
    h = x
    for layer in range(DEPTH):
        i = layer // 2
        hn = rmsnorm(h, norm_mix[layer])
        if layer % 2 == 0:
            h = h + even_mixer(hn, ev_w_in[i], ev_pool_w[i], ev_pool_scale[i], ev_w_out[i], pos)
            h = h + swiglu(rmsnorm(h, norm_ffn[layer]), ffn_w_gu[i], ffn_w_down[i])
        else:
            h = h + short_conv_mixer(hn, od_w_in[i], od_conv_w[i], od_w_out[i])
            h = h + moe_swiglu(rmsnorm(h, norm_ffn[layer]), moe_router[i], moe_w_gu[i], moe_w_down[i])
    return rmsnorm(h, norm_final)
```

```python
import functools
import math

import jax
import jax.numpy as jnp
from jax import lax
from jax.experimental import pallas as pl
from jax.experimental.pallas import tpu as pltpu

EPS = 1e-6
CHUNK = 64
POOL_WINDOWS = (2, 4, 8, 16)
POOL_GROUP_DIM = 256
RET_HEADS = 4
RET_HEAD_DIM = 256
ROPE_BASE = 10000.0
N_EXPERTS = 8
TOP_K = 2

RET_BLOCK = 256
POOL_HALO = 128
CONV_HALO = 16
ROUTER_LANES = 128

BF16 = jnp.bfloat16
F32 = jnp.float32
MIB = 1024 * 1024


def _params(semantics, vmem_mib):
    return pltpu.CompilerParams(dimension_semantics=semantics, vmem_limit_bytes=vmem_mib * MIB)


def _rms(x, g):
    ms = jnp.mean(x * x, axis=-1, keepdims=True)
    return x * lax.rsqrt(ms + EPS) * g


def _silu(x):
    return x / (1.0 + jnp.exp(-x))


def _norm_matmul_kernel(x_ref, g_ref, w_ref, o_ref, xn_ref):
    @pl.when(pl.program_id(1) == 0)
    def _():
        xn_ref[...] = _rms(x_ref[...], g_ref[...]).astype(xn_ref.dtype)

    o_ref[...] = jnp.dot(xn_ref[...], w_ref[...], preferred_element_type=F32).astype(o_ref.dtype)


def norm_matmul(x, g, w, *, tm=1024, tn=1024):
    t, d = x.shape
    n = w.shape[1]
    tm, tn = min(tm, t), min(tn, n)
    return pl.pallas_call(
        _norm_matmul_kernel,
        grid=(t // tm, n // tn),
        in_specs=[
            pl.BlockSpec((tm, d), lambda i, j: (i, 0)),
            pl.BlockSpec((1, d), lambda i, j: (0, 0)),
            pl.BlockSpec((d, tn), lambda i, j: (0, j)),
        ],
        out_specs=pl.BlockSpec((tm, tn), lambda i, j: (i, j)),
        out_shape=jax.ShapeDtypeStruct((t, n), BF16),
        scratch_shapes=[pltpu.VMEM((tm, d), BF16)],
        compiler_params=_params(("arbitrary", "arbitrary"), 48),
        name="norm_matmul",
    )(x, g.reshape(1, d), w)


def _even_core_kernel(u_ref, uh_ref, q_ref, k_ref, v_ref, g_ref, cos_ref, sin_ref, dm_ref,
                      wp_ref, ps_ref, y_ref, st_ref, *, ts, log_g):
    s = pl.program_id(1)
    gd = POOL_GROUP_DIM
    pw = len(POOL_WINDOWS) * gd

    @pl.when(s == 0)
    def _():
        st_ref[...] = jnp.zeros_like(st_ref)

    row = lax.broadcasted_iota(jnp.int32, (ts, ts), 0)
    col = lax.broadcasted_iota(jnp.int32, (ts, ts), 1)
    dist = row - col
    hrow = lax.broadcasted_iota(jnp.int32, (ts, POOL_HALO), 0)
    hcol = lax.broadcasted_iota(jnp.int32, (ts, POOL_HALO), 1)
    hdist = hrow - hcol + POOL_HALO
    t_seq = s * ts + lax.broadcasted_iota(jnp.int32, (ts, 1), 0)
    uh = jnp.where(s > 0, uh_ref[...], jnp.zeros_like(uh_ref))
    for g, w in enumerate(POOL_WINDOWS):
        cs = slice(g * gd, (g + 1) * gd)
        ug = u_ref[:, cs]
        band = jnp.logical_and(dist >= 0, dist < w).astype(BF16)
        hband = (hdist < w).astype(BF16)
        wsum = (jnp.dot(band, ug, preferred_element_type=F32)
                + jnp.dot(hband, uh[:, cs], preferred_element_type=F32))
        count = jnp.minimum(t_seq + 1, w).astype(F32)
        p = (wsum / count - ug.astype(F32)).astype(BF16)
        yg = jnp.dot(p, wp_ref[g], preferred_element_type=F32) * ps_ref[:, cs]
        y_ref[:, cs] = yg.astype(y_ref.dtype)

    lb = RET_BLOCK
    hd2 = RET_HEAD_DIM // 2
    n_idx = lax.broadcasted_iota(jnp.int32, (lb, 1), 0).astype(F32)
    for hd in range(RET_HEADS):
        lg = log_g[hd]
        q_decay = jnp.exp(lg * (n_idx + 1.0))
        k_decay = jnp.exp(lg * (lb - 1.0 - n_idx))
        block_decay = math.exp(lg * lb)
        c0 = hd * RET_HEAD_DIM
        for r in range(ts // lb):
            rows = slice(r * lb, (r + 1) * lb)
            cs_, sn_ = cos_ref[rows, :], sin_ref[rows, :]

            def rope(ref):
                x1 = ref[rows, c0:c0 + hd2].astype(F32)
                x2 = ref[rows, c0 + hd2:c0 + 2 * hd2].astype(F32)
                return jnp.concatenate([x1 * cs_ - x2 * sn_, x2 * cs_ + x1 * sn_], axis=-1)

            q = rope(q_ref) * (RET_HEAD_DIM ** -0.5)
            k = rope(k_ref)
            v = v_ref[rows, c0:c0 + RET_HEAD_DIM]
            scores = lax.dot_general(q.astype(BF16), k.astype(BF16), (((1,), (1,)), ((), ())),
                                     preferred_element_type=F32) * dm_ref[hd]
            state = st_ref[hd]
            o = (jnp.dot(scores.astype(BF16), v, preferred_element_type=F32)
                 + jnp.dot((q * q_decay).astype(BF16), state.astype(BF16), preferred_element_type=F32))
            st_ref[hd] = state * block_decay + lax.dot_general(
                (k * k_decay).astype(BF16), v, (((0,), (0,)), ((), ())), preferred_element_type=F32)
            mu = jnp.mean(o, axis=-1, keepdims=True)
            oc = o - mu
            var = jnp.mean(oc * oc, axis=-1, keepdims=True)
            gate = g_ref[rows, c0:c0 + RET_HEAD_DIM].astype(F32)
            y_ref[rows, pw + c0:pw + c0 + RET_HEAD_DIM] = (
                oc * lax.rsqrt(var + 1e-5) * _silu(gate)).astype(y_ref.dtype)


def even_core(proj, w_pool, pool_scale, batch, seq, *, ts=512):
    t = proj.shape[0]
    pw = len(POOL_WINDOWS) * POOL_GROUP_DIM
    rw = RET_HEADS * RET_HEAD_DIM
    assert pw == rw and proj.shape[1] == pw + 4 * rw
    ts = min(ts, seq)
    assert ts % RET_BLOCK == 0 and RET_BLOCK % CHUNK == 0 and ts % POOL_HALO == 0
    nsb = seq // ts
    half = RET_HEAD_DIM // 2

    pos = jnp.arange(seq, dtype=F32)
    inv = ROPE_BASE ** (-jnp.arange(half, dtype=F32) / half)
    ang = pos[:, None] * inv[None, :]
    cos, sin = jnp.cos(ang), jnp.sin(ang)
    log_g = tuple(math.log(1.0 - 2.0 ** (-5.0 - h)) for h in range(RET_HEADS))
    idx = jnp.arange(RET_BLOCK)
    visible = (idx[None, :] // CHUNK) <= (idx[:, None] // CHUNK)
    gap = jnp.abs(idx[:, None] - idx[None, :]).astype(F32)
    dmask = jnp.stack([jnp.where(visible, jnp.exp(lg * gap), 0.0) for lg in log_g]).astype(F32)

    def col(c):
        return pl.BlockSpec((ts, pw), lambda b, s: (b * nsb + s, c))

    halo = pl.BlockSpec(
        (POOL_HALO, pw), lambda b, s: (jnp.maximum((b * nsb + s) * (ts // POOL_HALO) - 1, 0), 0))
    return pl.pallas_call(
        functools.partial(_even_core_kernel, ts=ts, log_g=log_g),
        grid=(batch, nsb),
        in_specs=[
            col(0), halo, col(1), col(2), col(3), col(4),
            pl.BlockSpec((ts, half), lambda b, s: (s, 0)),
            pl.BlockSpec((ts, half), lambda b, s: (s, 0)),
            pl.BlockSpec((RET_HEADS, RET_BLOCK, RET_BLOCK), lambda b, s: (0, 0, 0)),
            pl.BlockSpec(w_pool.shape, lambda b, s: (0, 0, 0)),
            pl.BlockSpec((1, pw), lambda b, s: (0, 0)),
        ],
        out_specs=pl.BlockSpec((ts, pw + rw), lambda b, s: (b * nsb + s, 0)),
        out_shape=jax.ShapeDtypeStruct((t, pw + rw), BF16),
        scratch_shapes=[pltpu.VMEM((RET_HEADS, RET_HEAD_DIM, RET_HEAD_DIM), F32)],
        compiler_params=_params(("arbitrary", "arbitrary"), 48),
        name="even_core",
    )(proj, proj, proj, proj, proj, proj, cos, sin, dmask, w_pool, pool_scale.reshape(1, pw))


def _matmul_residual_kernel(y_ref, w_ref, h_ref, o_ref):
    o_ref[...] = h_ref[...] + jnp.dot(y_ref[...], w_ref[...], preferred_element_type=F32)


def matmul_residual(y, w, h, *, tm=512):
    t, k = y.shape
    d = w.shape[1]
    tm = min(tm, t)
    return pl.pallas_call(
        _matmul_residual_kernel,
        grid=(t // tm,),
        in_specs=[
            pl.BlockSpec((tm, k), lambda i: (i, 0)),
            pl.BlockSpec((k, d), lambda i: (0, 0)),
            pl.BlockSpec((tm, d), lambda i: (i, 0)),
        ],
        out_specs=pl.BlockSpec((tm, d), lambda i: (i, 0)),
        out_shape=jax.ShapeDtypeStruct((t, d), F32),
        compiler_params=_params(("arbitrary",), 48),
        name="matmul_residual",
    )(y, w, h)


def _ffn_kernel(x_ref, g_ref, wg_ref, wu_ref, wd_ref, o_ref, xn_ref):
    @pl.when(pl.program_id(1) == 0)
    def _():
        x = x_ref[...]
        xn_ref[...] = _rms(x, g_ref[...]).astype(xn_ref.dtype)
        o_ref[...] = x

    xn = xn_ref[...]
    gate = jnp.dot(xn, wg_ref[...], preferred_element_type=F32)
    up = jnp.dot(xn, wu_ref[...], preferred_element_type=F32)
    act = (_silu(gate) * up).astype(BF16)
    o_ref[...] += jnp.dot(act, wd_ref[...], preferred_element_type=F32)


def ffn(h, g, w_gu, w_down, *, tm=512, tf=512):
    t, d = h.shape
    f = w_down.shape[0]
    tm, tf = min(tm, t), min(tf, f)
    nj = f // tf
    return pl.pallas_call(
        _ffn_kernel,
        grid=(t // tm, nj),
        in_specs=[
            pl.BlockSpec((tm, d), lambda i, j: (i, 0)),
            pl.BlockSpec((1, d), lambda i, j: (0, 0)),
            pl.BlockSpec((d, tf), lambda i, j: (0, j)),
            pl.BlockSpec((d, tf), lambda i, j: (0, j + nj)),
            pl.BlockSpec((tf, d), lambda i, j: (j, 0)),
        ],
        out_specs=pl.BlockSpec((tm, d), lambda i, j: (i, 0)),
        out_shape=jax.ShapeDtypeStruct((t, d), F32),
        scratch_shapes=[pltpu.VMEM((tm, d), BF16)],
        compiler_params=_params(("arbitrary", "arbitrary"), 48),
        name="ffn",
    )(h, g.reshape(1, d), w_gu, w_gu, w_down)


def _conv_out_kernel(bg_ref, cg_ref, hx_ref, cgh_ref, hxh_ref, cw_ref, w_ref, h_ref, o_ref, *, ts):
    s = pl.program_id(1)
    z = cg_ref[...].astype(F32) * hx_ref[...].astype(F32)
    zh = cgh_ref[...].astype(F32) * hxh_ref[...].astype(F32)
    zh = jnp.where(s > 0, zh, 0.0)
    prev1 = zh[CONV_HALO - 1:CONV_HALO, :]
    prev2 = zh[CONV_HALO - 2:CONV_HALO - 1, :]
    row = lax.broadcasted_iota(jnp.int32, (ts, 1), 0)
    z1 = jnp.where(row == 0, prev1, pltpu.roll(z, 1, 0))
    z2 = jnp.where(row == 0, prev2, jnp.where(row == 1, prev1, pltpu.roll(z, 2, 0)))
    cw = cw_ref[...]
    conv = z2 * cw[0:1, :] + z1 * cw[1:2, :] + z * cw[2:3, :]
    act = (bg_ref[...].astype(F32) * conv).astype(BF16)
    o_ref[...] = h_ref[...] + jnp.dot(act, w_ref[...], preferred_element_type=F32)


def conv_out(proj, conv_w, w_out, h, batch, seq, *, ts=256):
    t, d = h.shape
    assert proj.shape[1] == 3 * d and conv_w.shape[0] == 3
    ts = min(ts, seq)
    nsb = seq // ts

    def col(c):
        return pl.BlockSpec((ts, d), lambda b, s: (b * nsb + s, c))

    def halo(c):
        return pl.BlockSpec(
            (CONV_HALO, d), lambda b, s: (jnp.maximum((b * nsb + s) * (ts // CONV_HALO) - 1, 0), c))

    return pl.pallas_call(
        functools.partial(_conv_out_kernel, ts=ts),
        grid=(batch, nsb),
        in_specs=[
            col(0), col(1), col(2), halo(1), halo(2),
            pl.BlockSpec(conv_w.shape, lambda b, s: (0, 0)),
            pl.BlockSpec(w_out.shape, lambda b, s: (0, 0)),
            pl.BlockSpec((ts, d), lambda b, s: (b * nsb + s, 0)),
        ],
        out_specs=pl.BlockSpec((ts, d), lambda b, s: (b * nsb + s, 0)),
        out_shape=jax.ShapeDtypeStruct((t, d), F32),
        compiler_params=_params(("arbitrary", "arbitrary"), 48),
        name="conv_out",
    )(proj, proj, proj, proj, proj, conv_w, w_out, h)


def _router_kernel(x_ref, g_ref, r_ref, xn_ref, idx_ref, wt_ref):
    xn = _rms(x_ref[...], g_ref[...])
    xn_ref[...] = xn
    logits = jnp.dot(xn, r_ref[...], preferred_element_type=F32, precision=lax.Precision.HIGHEST)
    lt = logits.T[:N_EXPERTS, :]
    e_id = lax.broadcasted_iota(jnp.int32, lt.shape, 0)
    m1 = jnp.max(lt, axis=0, keepdims=True)
    i1 = jnp.min(jnp.where(lt == m1, e_id, N_EXPERTS), axis=0, keepdims=True)
    rest = jnp.where(e_id == i1, -jnp.inf, lt)
    m2 = jnp.max(rest, axis=0, keepdims=True)
    i2 = jnp.min(jnp.where(rest == m2, e_id, N_EXPERTS), axis=0, keepdims=True)
    ex = jnp.exp(m2 - m1)
    w1 = 1.0 / (1.0 + ex)
    w2 = ex / (1.0 + ex)
    idx_ref[...] = jnp.where(e_id == 0, i1, jnp.where(e_id == 1, i2, 0))
    wt_ref[...] = jnp.where(e_id == 0, w1, jnp.where(e_id == 1, w2, 0.0))


def router(h, g, w_router, *, tm=512):
    t, d = h.shape
    e = w_router.shape[1]
    assert e == N_EXPERTS
    tm = min(tm, t)
    r_pad = jnp.zeros((d, ROUTER_LANES), F32).at[:, :e].set(w_router)
    return pl.pallas_call(
        _router_kernel,
        grid=(t // tm,),
        in_specs=[
            pl.BlockSpec((tm, d), lambda i: (i, 0)),
            pl.BlockSpec((1, d), lambda i: (0, 0)),
            pl.BlockSpec((d, ROUTER_LANES), lambda i: (0, 0)),
        ],
        out_specs=[
            pl.BlockSpec((tm, d), lambda i: (i, 0)),
            pl.BlockSpec((e, tm), lambda i: (0, i)),
            pl.BlockSpec((e, tm), lambda i: (0, i)),
        ],
        out_shape=[
            jax.ShapeDtypeStruct((t, d), F32),
            jax.ShapeDtypeStruct((e, t), jnp.int32),
            jax.ShapeDtypeStruct((e, t), F32),
        ],
        compiler_params=_params(("arbitrary",), 48),
        name="router",
    )(h, g.reshape(1, d), r_pad)


def _routing_tables(idx, tm):
    k, t = idx.shape
    n_tiles = (k * t) // tm + N_EXPERTS
    flat = idx.reshape(-1)
    onehot = (flat[:, None] == jnp.arange(N_EXPERTS)[None, :]).astype(jnp.int32)
    csum = jnp.cumsum(onehot, axis=0)
    rank = jnp.sum(onehot * (csum - 1), axis=1)
    counts = csum[-1]
    padded = ((counts + tm - 1) // tm) * tm
    g_end = jnp.cumsum(padded)
    g_start = g_end - padded
    pos = jnp.sum(onehot * g_start[None, :], axis=1) + rank
    tok = jnp.tile(jnp.arange(t, dtype=jnp.int32), k)
    row_token = jnp.zeros((n_tiles * tm,), jnp.int32).at[pos].set(tok)
    n_active = (g_end[-1] // tm).astype(jnp.int32)
    tile_start = jnp.arange(n_tiles, dtype=jnp.int32) * tm
    tile_expert = jnp.sum((tile_start[:, None] >= g_end[None, :]).astype(jnp.int32), axis=1)
    tile_expert = jnp.minimum(tile_expert, N_EXPERTS - 1)
    last = tile_expert[n_active - 1]
    tile_expert = jnp.where(jnp.arange(n_tiles) < n_active, tile_expert, last).astype(jnp.int32)
    return pos.reshape(k, t).astype(jnp.int32), row_token, tile_expert, n_active.reshape(1)


def _moe_kernel(te_ref, rt_ref, na_ref, x_hbm, wg_ref, wu_ref, wd_ref, o_ref, xbuf, xb, sem, *, tm):
    i = pl.program_id(0)
    j = pl.program_id(1)
    n_active = na_ref[0]
    active = i < n_active

    def start_gather(tile, slot):
        base = tile * tm

        def body(r, carry):
            tok = rt_ref[base + r]
            pltpu.make_async_copy(x_hbm.at[pl.ds(tok, 1)], xbuf.at[slot, pl.ds(r, 1)], sem.at[slot]).start()
            return carry

        lax.fori_loop(0, tm, body, 0)

    @pl.when(j == 0)
    def _():
        @pl.when(i == 0)
        def _():
            start_gather(0, 0)

        @pl.when(active)
        def _():
            slot = i % 2
            pltpu.make_async_copy(x_hbm.at[pl.ds(0, tm)], xbuf.at[slot], sem.at[slot]).wait()

            @pl.when(i + 1 < n_active)
            def _():
                start_gather(i + 1, (i + 1) % 2)

            xb[...] = xbuf[slot].astype(BF16)

        o_ref[...] = jnp.zeros_like(o_ref)

    @pl.when(active)
    def _():
        x = xb[...]
        gate = jnp.dot(x, wg_ref[...], preferred_element_type=F32)
        up = jnp.dot(x, wu_ref[...], preferred_element_type=F32)
        act = (_silu(gate) * up).astype(BF16)
        o_ref[...] += jnp.dot(act, wd_ref[...], preferred_element_type=F32)


def moe_experts(xn, row_token, tile_expert, n_active, w_gu, w_down, *, tm, tf=256):
    t, d = xn.shape
    f = w_down.shape[1]
    tf = min(tf, f)
    nj = f // tf
    n_tiles = tile_expert.shape[0]

    def jj(i, j, na):
        return jnp.where(i < na[0], j, nj - 1)

    grid_spec = pltpu.PrefetchScalarGridSpec(
        num_scalar_prefetch=3,
        grid=(n_tiles, nj),
        in_specs=[
            pl.BlockSpec(memory_space=pl.ANY),
            pl.BlockSpec((None, d, tf), lambda i, j, te, rt, na: (te[i], 0, jj(i, j, na))),
            pl.BlockSpec((None, d, tf), lambda i, j, te, rt, na: (te[i], 0, jj(i, j, na) + nj)),
            pl.BlockSpec((None, tf, d), lambda i, j, te, rt, na: (te[i], jj(i, j, na), 0)),
        ],
        out_specs=pl.BlockSpec((tm, d), lambda i, j, te, rt, na: (i, 0)),
        scratch_shapes=[
            pltpu.VMEM((2, tm, d), F32),
            pltpu.VMEM((tm, d), BF16),
            pltpu.SemaphoreType.DMA((2,)),
        ],
    )
    return pl.pallas_call(
        functools.partial(_moe_kernel, tm=tm),
        grid_spec=grid_spec,
        out_shape=jax.ShapeDtypeStruct((n_tiles * tm, d), F32),
        compiler_params=_params(("arbitrary", "arbitrary"), 48),
        name="moe_experts",
    )(tile_expert, row_token, n_active, xn, w_gu, w_gu, w_down)


def _combine_kernel(p0_ref, p1_ref, ys_hbm, h_ref, wt_ref, gf_ref, o_ref, buf, sem, *, tc, final_norm):
    i = pl.program_id(0)
    n = pl.num_programs(0)

    def start_gather(tile, slot):
        base = tile * tc

        def body(r, carry):
            pltpu.make_async_copy(
                ys_hbm.at[pl.ds(p0_ref[base + r], 1)], buf.at[slot, 0, pl.ds(r, 1)], sem.at[slot]).start()
            pltpu.make_async_copy(
                ys_hbm.at[pl.ds(p1_ref[base + r], 1)], buf.at[slot, 1, pl.ds(r, 1)], sem.at[slot]).start()
            return carry

        lax.fori_loop(0, tc, body, 0)

    @pl.when(i == 0)
    def _():
        start_gather(0, 0)

    slot = i % 2
    for kk in range(TOP_K):
        pltpu.make_async_copy(ys_hbm.at[pl.ds(0, tc)], buf.at[slot, kk], sem.at[slot]).wait()

    @pl.when(i + 1 < n)
    def _():
        start_gather(i + 1, (i + 1) % 2)

    wt = wt_ref[...].T
    y = wt[:, 0:1] * buf[slot, 0] + wt[:, 1:2] * buf[slot, 1]
    out = h_ref[...] + y
    if final_norm:
        out = _rms(out, gf_ref[...])
    o_ref[...] = out


def moe_combine(ys, pos, wts, h, final_gain, *, tc=256):
    t, d = h.shape
    tc = min(tc, t)
    final_norm = final_gain is not None
    gf = (final_gain if final_norm else jnp.ones((d,), F32)).reshape(1, d)
    grid_spec = pltpu.PrefetchScalarGridSpec(
        num_scalar_prefetch=2,
        grid=(t // tc,),
        in_specs=[
            pl.BlockSpec(memory_space=pl.ANY),
            pl.BlockSpec((tc, d), lambda i, p0, p1: (i, 0)),
            pl.BlockSpec((N_EXPERTS, tc), lambda i, p0, p1: (0, i)),
            pl.BlockSpec((1, d), lambda i, p0, p1: (0, 0)),
        ],
        out_specs=pl.BlockSpec((tc, d), lambda i, p0, p1: (i, 0)),
        scratch_shapes=[
            pltpu.VMEM((2, TOP_K, tc, d), F32),
            pltpu.SemaphoreType.DMA((2,)),
        ],
    )
    return pl.pallas_call(
        functools.partial(_combine_kernel, tc=tc, final_norm=final_norm),
        grid_spec=grid_spec,
        out_shape=jax.ShapeDtypeStruct((t, d), F32),
        compiler_params=_params(("arbitrary",), 48),
        name="moe_combine",
    )(pos[0], pos[1], ys, h, wts, gf)


def moe_block(h, g, w_router, w_gu, w_down, final_gain, *, tm=512):
    t = h.shape[0]
    tm = min(tm, t)
    xn, idx, wts = router(h, g, w_router)
    pos, row_token, tile_expert, n_active = _routing_tables(idx[:TOP_K], tm)
    ys = moe_experts(xn, row_token, tile_expert, n_active, w_gu, w_down, tm=tm)
    return moe_combine(ys, pos, wts, h, final_gain)


def kernel(x, norm_mix, norm_ffn, norm_final, ev_w_in, ev_pool_w, ev_pool_scale, ev_w_out,
           od_w_in, od_conv_w, od_w_out, ffn_w_gu, ffn_w_down, moe_router, moe_w_gu, moe_w_down):
    batch, seq, d = x.shape
    depth = norm_mix.shape[0]
    h = x.reshape(batch * seq, d)
    for layer in range(depth):
        i = layer // 2
        if layer % 2 == 0:
            proj = norm_matmul(h, norm_mix[layer], ev_w_in[i].astype(BF16))
            y = even_core(proj, ev_pool_w[i].astype(BF16), ev_pool_scale[i], batch, seq)
            h = matmul_residual(y, ev_w_out[i].astype(BF16), h)
            h = ffn(h, norm_ffn[layer], ffn_w_gu[i].astype(BF16), ffn_w_down[i].astype(BF16))
        else:
            proj = norm_matmul(h, norm_mix[layer], od_w_in[i].astype(BF16))
            h = conv_out(proj, od_conv_w[i], od_w_out[i].astype(BF16), h, batch, seq)
            final_gain = norm_final if layer == depth - 1 else None
            h = moe_block(h, norm_ffn[layer], moe_router[i], moe_w_gu[i].astype(BF16),
                          moe_w_down[i].astype(BF16), final_gain)
    return h.reshape(batch, seq, d)
```

```python
import functools
import math

import jax
import jax.numpy as jnp
from jax import lax
from jax.experimental import pallas as pl
from jax.experimental.pallas import tpu as pltpu

EPS = 1e-6
CHUNK = 64
POOL_WINDOWS = (2, 4, 8, 16)
POOL_GROUP_DIM = 256
RET_HEADS = 4
RET_HEAD_DIM = 256
ROPE_BASE = 10000.0
N_EXPERTS = 8
TOP_K = 2

RET_BLOCK = 256
POOL_HALO = 128
CONV_HALO = 16
ROUTER_LANES = 128
GATHER_UNROLL = 8

BF16 = jnp.bfloat16
F32 = jnp.float32
MIB = 1024 * 1024


def _params(semantics, vmem_mib):
    return pltpu.CompilerParams(dimension_semantics=semantics, vmem_limit_bytes=vmem_mib * MIB)


def _rms(x, g):
    ms = jnp.mean(x * x, axis=-1, keepdims=True)
    return x * lax.rsqrt(ms + EPS) * g


def _silu(x):
    return x / (1.0 + jnp.exp(-x))


def _norm_matmul_kernel(x_ref, g_ref, w_ref, o_ref, xn_ref):
    @pl.when(pl.program_id(1) == 0)
    def _():
        xn_ref[...] = _rms(x_ref[...], g_ref[...]).astype(xn_ref.dtype)

    o_ref[...] = jnp.dot(xn_ref[...], w_ref[...], preferred_element_type=F32).astype(o_ref.dtype)


def norm_matmul(x, g, w, *, tm=1024, tn=1024):
    t, d = x.shape
    n = w.shape[1]
    tm, tn = min(tm, t), min(tn, n)
    return pl.pallas_call(
        _norm_matmul_kernel,
        grid=(t // tm, n // tn),
        in_specs=[
            pl.BlockSpec((tm, d), lambda i, j: (i, 0)),
            pl.BlockSpec((1, d), lambda i, j: (0, 0)),
            pl.BlockSpec((d, tn), lambda i, j: (0, j)),
        ],
        out_specs=pl.BlockSpec((tm, tn), lambda i, j: (i, j)),
        out_shape=jax.ShapeDtypeStruct((t, n), BF16),
        scratch_shapes=[pltpu.VMEM((tm, d), BF16)],
        compiler_params=_params(("arbitrary", "arbitrary"), 48),
        name="norm_matmul",
    )(x, g.reshape(1, d), w)


def _even_core_kernel(u_ref, uh_ref, q_ref, k_ref, v_ref, g_ref, cos_ref, sin_ref, dm_ref,
                      wp_ref, ps_ref, y_ref, st_ref, *, ts, log_g):
    s = pl.program_id(1)
    gd = POOL_GROUP_DIM
    pw = len(POOL_WINDOWS) * gd

    @pl.when(s == 0)
    def _():
        st_ref[...] = jnp.zeros_like(st_ref)

    row = lax.broadcasted_iota(jnp.int32, (ts, ts), 0)
    col = lax.broadcasted_iota(jnp.int32, (ts, ts), 1)
    dist = row - col
    hrow = lax.broadcasted_iota(jnp.int32, (ts, POOL_HALO), 0)
    hcol = lax.broadcasted_iota(jnp.int32, (ts, POOL_HALO), 1)
    hdist = hrow - hcol + POOL_HALO
    t_seq = s * ts + lax.broadcasted_iota(jnp.int32, (ts, 1), 0)
    uh = jnp.where(s > 0, uh_ref[...], jnp.zeros_like(uh_ref))
    for g, w in enumerate(POOL_WINDOWS):
        cs = slice(g * gd, (g + 1) * gd)
        ug = u_ref[:, cs]
        band = jnp.logical_and(dist >= 0, dist < w).astype(BF16)
        hband = (hdist < w).astype(BF16)
        wsum = (jnp.dot(band, ug, preferred_element_type=F32)
                + jnp.dot(hband, uh[:, cs], preferred_element_type=F32))
        count = jnp.minimum(t_seq + 1, w).astype(F32)
        p = (wsum / count - ug.astype(F32)).astype(BF16)
        yg = jnp.dot(p, wp_ref[g], preferred_element_type=F32) * ps_ref[:, cs]
        y_ref[:, cs] = yg.astype(y_ref.dtype)

    lb = RET_BLOCK
    hd2 = RET_HEAD_DIM // 2
    n_idx = lax.broadcasted_iota(jnp.int32, (lb, 1), 0).astype(F32)
    for hd in range(RET_HEADS):
        lg = log_g[hd]
        q_decay = jnp.exp(lg * (n_idx + 1.0))
        k_decay = jnp.exp(lg * (lb - 1.0 - n_idx))
        block_decay = math.exp(lg * lb)
        c0 = hd * RET_HEAD_DIM
        for r in range(ts // lb):
            rows = slice(r * lb, (r + 1) * lb)
            cs_, sn_ = cos_ref[rows, :], sin_ref[rows, :]

            def rope(ref):
                x1 = ref[rows, c0:c0 + hd2].astype(F32)
                x2 = ref[rows, c0 + hd2:c0 + 2 * hd2].astype(F32)
                return jnp.concatenate([x1 * cs_ - x2 * sn_, x2 * cs_ + x1 * sn_], axis=-1)

            q = rope(q_ref) * (RET_HEAD_DIM ** -0.5)
            k = rope(k_ref)
            v = v_ref[rows, c0:c0 + RET_HEAD_DIM]
            scores = lax.dot_general(q.astype(BF16), k.astype(BF16), (((1,), (1,)), ((), ())),
                                     preferred_element_type=F32) * dm_ref[hd]
            state = st_ref[hd]
            o = (jnp.dot(scores.astype(BF16), v, preferred_element_type=F32)
                 + jnp.dot((q * q_decay).astype(BF16), state.astype(BF16), preferred_element_type=F32))
            st_ref[hd] = state * block_decay + lax.dot_general(
                (k * k_decay).astype(BF16), v, (((0,), (0,)), ((), ())), preferred_element_type=F32)
            mu = jnp.mean(o, axis=-1, keepdims=True)
            oc = o - mu
            var = jnp.mean(oc * oc, axis=-1, keepdims=True)
            gate = g_ref[rows, c0:c0 + RET_HEAD_DIM].astype(F32)
            y_ref[rows, pw + c0:pw + c0 + RET_HEAD_DIM] = (
                oc * lax.rsqrt(var + 1e-5) * _silu(gate)).astype(y_ref.dtype)


def even_core(proj, w_pool, pool_scale, batch, seq, *, ts=512):
    t = proj.shape[0]
    pw = len(POOL_WINDOWS) * POOL_GROUP_DIM
    rw = RET_HEADS * RET_HEAD_DIM
    assert pw == rw and proj.shape[1] == pw + 4 * rw
    ts = min(ts, seq)
    assert ts % RET_BLOCK == 0 and RET_BLOCK % CHUNK == 0 and ts % POOL_HALO == 0
    nsb = seq // ts
    half = RET_HEAD_DIM // 2

    pos = jnp.arange(seq, dtype=F32)
    inv = ROPE_BASE ** (-jnp.arange(half, dtype=F32) / half)
    ang = pos[:, None] * inv[None, :]
    cos, sin = jnp.cos(ang), jnp.sin(ang)
    log_g = tuple(math.log(1.0 - 2.0 ** (-5.0 - h)) for h in range(RET_HEADS))
    idx = jnp.arange(RET_BLOCK)
    visible = (idx[None, :] // CHUNK) <= (idx[:, None] // CHUNK)
    gap = jnp.abs(idx[:, None] - idx[None, :]).astype(F32)
    dmask = jnp.stack([jnp.where(visible, jnp.exp(lg * gap), 0.0) for lg in log_g]).astype(F32)

    def col(c):
        return pl.BlockSpec((ts, pw), lambda b, s: (b * nsb + s, c))

    halo = pl.BlockSpec(
        (POOL_HALO, pw), lambda b, s: (jnp.maximum((b * nsb + s) * (ts // POOL_HALO) - 1, 0), 0))
    return pl.pallas_call(
        functools.partial(_even_core_kernel, ts=ts, log_g=log_g),
        grid=(batch, nsb),
        in_specs=[
            col(0), halo, col(1), col(2), col(3), col(4),
            pl.BlockSpec((ts, half), lambda b, s: (s, 0)),
            pl.BlockSpec((ts, half), lambda b, s: (s, 0)),
            pl.BlockSpec((RET_HEADS, RET_BLOCK, RET_BLOCK), lambda b, s: (0, 0, 0)),
            pl.BlockSpec(w_pool.shape, lambda b, s: (0, 0, 0)),
            pl.BlockSpec((1, pw), lambda b, s: (0, 0)),
        ],
        out_specs=pl.BlockSpec((ts, pw + rw), lambda b, s: (b * nsb + s, 0)),
        out_shape=jax.ShapeDtypeStruct((t, pw + rw), BF16),
        scratch_shapes=[pltpu.VMEM((RET_HEADS, RET_HEAD_DIM, RET_HEAD_DIM), F32)],
        compiler_params=_params(("arbitrary", "arbitrary"), 48),
        name="even_core",
    )(proj, proj, proj, proj, proj, proj, cos, sin, dmask, w_pool, pool_scale.reshape(1, pw))


def _matmul_residual_kernel(y_ref, w_ref, h_ref, o_ref):
    o_ref[...] = h_ref[...] + jnp.dot(y_ref[...], w_ref[...], preferred_element_type=F32)


def matmul_residual(y, w, h, *, tm=512):
    t, k = y.shape
    d = w.shape[1]
    tm = min(tm, t)
    return pl.pallas_call(
        _matmul_residual_kernel,
        grid=(t // tm,),
        in_specs=[
            pl.BlockSpec((tm, k), lambda i: (i, 0)),
            pl.BlockSpec((k, d), lambda i: (0, 0)),
            pl.BlockSpec((tm, d), lambda i: (i, 0)),
        ],
        out_specs=pl.BlockSpec((tm, d), lambda i: (i, 0)),
        out_shape=jax.ShapeDtypeStruct((t, d), F32),
        compiler_params=_params(("arbitrary",), 48),
        name="matmul_residual",
    )(y, w, h)


def _ffn_kernel(x_ref, g_ref, wg_ref, wu_ref, wd_ref, o_ref, xn_ref):
    @pl.when(pl.program_id(1) == 0)
    def _():
        x = x_ref[...]
        xn_ref[...] = _rms(x, g_ref[...]).astype(xn_ref.dtype)
        o_ref[...] = x

    xn = xn_ref[...]
    gate = jnp.dot(xn, wg_ref[...], preferred_element_type=F32)
    up = jnp.dot(xn, wu_ref[...], preferred_element_type=F32)
    act = (_silu(gate) * up).astype(BF16)
    o_ref[...] += jnp.dot(act, wd_ref[...], preferred_element_type=F32)


def ffn(h, g, w_gu, w_down, *, tm=512, tf=512):
    t, d = h.shape
    f = w_down.shape[0]
    tm, tf = min(tm, t), min(tf, f)
    nj = f // tf
    return pl.pallas_call(
        _ffn_kernel,
        grid=(t // tm, nj),
        in_specs=[
            pl.BlockSpec((tm, d), lambda i, j: (i, 0)),
            pl.BlockSpec((1, d), lambda i, j: (0, 0)),
            pl.BlockSpec((d, tf), lambda i, j: (0, j)),
            pl.BlockSpec((d, tf), lambda i, j: (0, j + nj)),
            pl.BlockSpec((tf, d), lambda i, j: (j, 0)),
        ],
        out_specs=pl.BlockSpec((tm, d), lambda i, j: (i, 0)),
        out_shape=jax.ShapeDtypeStruct((t, d), F32),
        scratch_shapes=[pltpu.VMEM((tm, d), BF16)],
        compiler_params=_params(("arbitrary", "arbitrary"), 48),
        name="ffn",
    )(h, g.reshape(1, d), w_gu, w_gu, w_down)


def _conv_out_kernel(bg_ref, cg_ref, hx_ref, cgh_ref, hxh_ref, cw_ref, w_ref, h_ref, o_ref, *, ts):
    s = pl.program_id(1)
    z = cg_ref[...].astype(F32) * hx_ref[...].astype(F32)
    zh = cgh_ref[...].astype(F32) * hxh_ref[...].astype(F32)
    zh = jnp.where(s > 0, zh, 0.0)
    prev1 = zh[CONV_HALO - 1:CONV_HALO, :]
    prev2 = zh[CONV_HALO - 2:CONV_HALO - 1, :]
    row = lax.broadcasted_iota(jnp.int32, (ts, 1), 0)
    z1 = jnp.where(row == 0, prev1, pltpu.roll(z, 1, 0))
    z2 = jnp.where(row == 0, prev2, jnp.where(row == 1, prev1, pltpu.roll(z, 2, 0)))
    cw = cw_ref[...]
    conv = z2 * cw[0:1, :] + z1 * cw[1:2, :] + z * cw[2:3, :]
    act = (bg_ref[...].astype(F32) * conv).astype(BF16)
    o_ref[...] = h_ref[...] + jnp.dot(act, w_ref[...], preferred_element_type=F32)


def conv_out(proj, conv_w, w_out, h, batch, seq, *, ts=256):
    t, d = h.shape
    assert proj.shape[1] == 3 * d and conv_w.shape[0] == 3
    ts = min(ts, seq)
    nsb = seq // ts

    def col(c):
        return pl.BlockSpec((ts, d), lambda b, s: (b * nsb + s, c))

    def halo(c):
        return pl.BlockSpec(
            (CONV_HALO, d), lambda b, s: (jnp.maximum((b * nsb + s) * (ts // CONV_HALO) - 1, 0), c))

    return pl.pallas_call(
        functools.partial(_conv_out_kernel, ts=ts),
        grid=(batch, nsb),
        in_specs=[
            col(0), col(1), col(2), halo(1), halo(2),
            pl.BlockSpec(conv_w.shape, lambda b, s: (0, 0)),
            pl.BlockSpec(w_out.shape, lambda b, s: (0, 0)),
            pl.BlockSpec((ts, d), lambda b, s: (b * nsb + s, 0)),
        ],
        out_specs=pl.BlockSpec((ts, d), lambda b, s: (b * nsb + s, 0)),
        out_shape=jax.ShapeDtypeStruct((t, d), F32),
        compiler_params=_params(("arbitrary", "arbitrary"), 48),
        name="conv_out",
    )(proj, proj, proj, proj, proj, conv_w, w_out, h)


def _pack_bf16_pairs(x):
    n = x.shape[1] // 2
    hi = lax.bitcast_convert_type(x[:, :n].astype(BF16).astype(F32), jnp.uint32)
    lo = lax.bitcast_convert_type(x[:, n:].astype(BF16).astype(F32), jnp.uint32)
    return hi | (lo >> 16)


def _unpack_bf16_pairs(w):
    hi = lax.bitcast_convert_type(w & jnp.uint32(0xFFFF0000), F32).astype(BF16)
    lo = lax.bitcast_convert_type(w << 16, F32).astype(BF16)
    return hi, lo


def _router_kernel(x_ref, g_ref, r_ref, xn_ref, idx_ref, wt_ref):
    xn = _rms(x_ref[...], g_ref[...])
    xn_ref[...] = _pack_bf16_pairs(xn)
    logits = jnp.dot(xn, r_ref[...], preferred_element_type=F32, precision=lax.Precision.HIGHEST)
    lt = logits.T[:N_EXPERTS, :]
    e_id = lax.broadcasted_iota(jnp.int32, lt.shape, 0)
    m1 = jnp.max(lt, axis=0, keepdims=True)
    i1 = jnp.min(jnp.where(lt == m1, e_id, N_EXPERTS), axis=0, keepdims=True)
    rest = jnp.where(e_id == i1, -jnp.inf, lt)
    m2 = jnp.max(rest, axis=0, keepdims=True)
    i2 = jnp.min(jnp.where(rest == m2, e_id, N_EXPERTS), axis=0, keepdims=True)
    ex = jnp.exp(m2 - m1)
    w1 = 1.0 / (1.0 + ex)
    w2 = ex / (1.0 + ex)
    idx_ref[...] = jnp.where(e_id == 0, i1, jnp.where(e_id == 1, i2, 0))
    wt_ref[...] = jnp.where(e_id == 0, w1, jnp.where(e_id == 1, w2, 0.0))


def router(h, g, w_router, *, tm=512):
    t, d = h.shape
    e = w_router.shape[1]
    assert e == N_EXPERTS
    tm = min(tm, t)
    r_pad = jnp.zeros((d, ROUTER_LANES), F32).at[:, :e].set(w_router)
    return pl.pallas_call(
        _router_kernel,
        grid=(t // tm,),
        in_specs=[
            pl.BlockSpec((tm, d), lambda i: (i, 0)),
            pl.BlockSpec((1, d), lambda i: (0, 0)),
            pl.BlockSpec((d, ROUTER_LANES), lambda i: (0, 0)),
        ],
        out_specs=[
            pl.BlockSpec((tm, d // 2), lambda i: (i, 0)),
            pl.BlockSpec((e, tm), lambda i: (0, i)),
            pl.BlockSpec((e, tm), lambda i: (0, i)),
        ],
        out_shape=[
            jax.ShapeDtypeStruct((t, d // 2), jnp.uint32),
            jax.ShapeDtypeStruct((e, t), jnp.int32),
            jax.ShapeDtypeStruct((e, t), F32),
        ],
        compiler_params=_params(("arbitrary",), 48),
        name="router",
    )(h, g.reshape(1, d), r_pad)


def _routing_tables(idx, tm):
    k, t = idx.shape
    n_tiles = (k * t) // tm + N_EXPERTS
    flat = idx.reshape(-1)
    onehot = (flat[:, None] == jnp.arange(N_EXPERTS)[None, :]).astype(jnp.int32)
    csum = jnp.cumsum(onehot, axis=0)
    rank = jnp.sum(onehot * (csum - 1), axis=1)
    counts = csum[-1]
    padded = ((counts + tm - 1) // tm) * tm
    g_end = jnp.cumsum(padded)
    g_start = g_end - padded
    pos = jnp.sum(onehot * g_start[None, :], axis=1) + rank
    tok = jnp.tile(jnp.arange(t, dtype=jnp.int32), k)
    row_token = jnp.zeros((n_tiles * tm,), jnp.int32).at[pos].set(tok)
    n_active = (g_end[-1] // tm).astype(jnp.int32)
    tile_start = jnp.arange(n_tiles, dtype=jnp.int32) * tm
    tile_expert = jnp.sum((tile_start[:, None] >= g_end[None, :]).astype(jnp.int32), axis=1)
    tile_expert = jnp.minimum(tile_expert, N_EXPERTS - 1)
    last = tile_expert[n_active - 1]
    tile_expert = jnp.where(jnp.arange(n_tiles) < n_active, tile_expert, last).astype(jnp.int32)
    return pos.reshape(k, t).astype(jnp.int32), row_token, tile_expert, n_active.reshape(1)


def _moe_kernel(te_ref, rt_ref, na_ref, x_hbm, wg_ref, wu_ref, wd_ref, o_ref, xbuf, xb, sem, *, tm):
    i = pl.program_id(0)
    j = pl.program_id(1)
    n_active = na_ref[0]
    active = i < n_active

    def start_gather(tile, slot):
        base = tile * tm

        def body(r, carry):
            tok = rt_ref[base + r]
            pltpu.make_async_copy(x_hbm.at[pl.ds(tok, 1)], xbuf.at[slot, pl.ds(r, 1)], sem.at[slot]).start()
            return carry

        lax.fori_loop(0, tm, body, 0, unroll=GATHER_UNROLL)

    @pl.when(j == 0)
    def _():
        @pl.when(i == 0)
        def _():
            start_gather(0, 0)

        @pl.when(active)
        def _():
            slot = i % 2
            pltpu.make_async_copy(x_hbm.at[pl.ds(0, tm)], xbuf.at[slot], sem.at[slot]).wait()

            @pl.when(i + 1 < n_active)
            def _():
                start_gather(i + 1, (i + 1) % 2)

            half = xbuf.shape[2]
            hi, lo = _unpack_bf16_pairs(xbuf[slot])
            xb[:, :half] = hi
            xb[:, half:] = lo

        o_ref[...] = jnp.zeros_like(o_ref)

    @pl.when(active)
    def _():
        x = xb[...]
        gate = jnp.dot(x, wg_ref[...].astype(BF16), preferred_element_type=F32)
        up = jnp.dot(x, wu_ref[...].astype(BF16), preferred_element_type=F32)
        act = (_silu(gate) * up).astype(BF16)
        o_ref[...] += jnp.dot(act, wd_ref[...].astype(BF16), preferred_element_type=F32)


def moe_experts(xn, row_token, tile_expert, n_active, w_gu, w_down, *, tm, tf=256):
    t, half = xn.shape
    d = 2 * half
    f = w_down.shape[1]
    tf = min(tf, f)
    nj = f // tf
    n_tiles = tile_expert.shape[0]

    def jj(i, j, na):
        return jnp.where(i < na[0], j, nj - 1)

    grid_spec = pltpu.PrefetchScalarGridSpec(
        num_scalar_prefetch=3,
        grid=(n_tiles, nj),
        in_specs=[
            pl.BlockSpec(memory_space=pl.ANY),
            pl.BlockSpec((None, d, tf), lambda i, j, te, rt, na: (te[i], 0, jj(i, j, na))),
            pl.BlockSpec((None, d, tf), lambda i, j, te, rt, na: (te[i], 0, jj(i, j, na) + nj)),
            pl.BlockSpec((None, tf, d), lambda i, j, te, rt, na: (te[i], jj(i, j, na), 0)),
        ],
        out_specs=pl.BlockSpec((tm, d), lambda i, j, te, rt, na: (i, 0)),
        scratch_shapes=[
            pltpu.VMEM((2, tm, half), jnp.uint32),
            pltpu.VMEM((tm, d), BF16),
            pltpu.SemaphoreType.DMA((2,)),
        ],
    )
    return pl.pallas_call(
        functools.partial(_moe_kernel, tm=tm),
        grid_spec=grid_spec,
        out_shape=jax.ShapeDtypeStruct((n_tiles * tm, d), F32),
        compiler_params=_params(("arbitrary", "arbitrary"), 56),
        name="moe_experts",
    )(tile_expert, row_token, n_active, xn, w_gu, w_gu, w_down)


def _combine_kernel(p0_ref, p1_ref, ys_hbm, h_ref, wt_ref, gf_ref, o_ref, buf, sem, *, tc, final_norm):
    i = pl.program_id(0)
    n = pl.num_programs(0)

    def start_gather(tile, slot):
        base = tile * tc

        def body(r, carry):
            pltpu.make_async_copy(
                ys_hbm.at[pl.ds(p0_ref[base + r], 1)], buf.at[slot, 0, pl.ds(r, 1)], sem.at[slot]).start()
            pltpu.make_async_copy(
                ys_hbm.at[pl.ds(p1_ref[base + r], 1)], buf.at[slot, 1, pl.ds(r, 1)], sem.at[slot]).start()
            return carry

        lax.fori_loop(0, tc, body, 0, unroll=GATHER_UNROLL)

    @pl.when(i == 0)
    def _():
        start_gather(0, 0)

    slot = i % 2
    for kk in range(TOP_K):
        pltpu.make_async_copy(ys_hbm.at[pl.ds(0, tc)], buf.at[slot, kk], sem.at[slot]).wait()

    @pl.when(i + 1 < n)
    def _():
        start_gather(i + 1, (i + 1) % 2)

    wt = wt_ref[...].T
    y = wt[:, 0:1] * buf[slot, 0] + wt[:, 1:2] * buf[slot, 1]
    out = h_ref[...] + y
    if final_norm:
        out = _rms(out, gf_ref[...])
    o_ref[...] = out


def moe_combine(ys, pos, wts, h, final_gain, *, tc=256):
    t, d = h.shape
    tc = min(tc, t)
    final_norm = final_gain is not None
    gf = (final_gain if final_norm else jnp.ones((d,), F32)).reshape(1, d)
    grid_spec = pltpu.PrefetchScalarGridSpec(
        num_scalar_prefetch=2,
        grid=(t // tc,),
        in_specs=[
            pl.BlockSpec(memory_space=pl.ANY),
            pl.BlockSpec((tc, d), lambda i, p0, p1: (i, 0)),
            pl.BlockSpec((N_EXPERTS, tc), lambda i, p0, p1: (0, i)),
            pl.BlockSpec((1, d), lambda i, p0, p1: (0, 0)),
        ],
        out_specs=pl.BlockSpec((tc, d), lambda i, p0, p1: (i, 0)),
        scratch_shapes=[
            pltpu.VMEM((2, TOP_K, tc, d), F32),
            pltpu.SemaphoreType.DMA((2,)),
        ],
    )
    return pl.pallas_call(
        functools.partial(_combine_kernel, tc=tc, final_norm=final_norm),
        grid_spec=grid_spec,
        out_shape=jax.ShapeDtypeStruct((t, d), F32),
        compiler_params=_params(("arbitrary",), 48),
        name="moe_combine",
    )(pos[0], pos[1], ys, h, wts, gf)


def moe_block(h, g, w_router, w_gu, w_down, final_gain, *, tm=1024):
    t = h.shape[0]
    tm = min(tm, t)
    xn, idx, wts = router(h, g, w_router)
    pos, row_token, tile_expert, n_active = _routing_tables(idx[:TOP_K], tm)
    ys = moe_experts(xn, row_token, tile_expert, n_active, w_gu, w_down, tm=tm)
    return moe_combine(ys, pos, wts, h, final_gain)


def kernel(x, norm_mix, norm_ffn, norm_final, ev_w_in, ev_pool_w, ev_pool_scale, ev_w_out,
           od_w_in, od_conv_w, od_w_out, ffn_w_gu, ffn_w_down, moe_router, moe_w_gu, moe_w_down):
    batch, seq, d = x.shape
    depth = norm_mix.shape[0]
    h = x.reshape(batch * seq, d)
    for layer in range(depth):
        i = layer // 2
        if layer % 2 == 0:
            proj = norm_matmul(h, norm_mix[layer], ev_w_in[i].astype(BF16))
            y = even_core(proj, ev_pool_w[i].astype(BF16), ev_pool_scale[i], batch, seq)
            h = matmul_residual(y, ev_w_out[i].astype(BF16), h)
            h = ffn(h, norm_ffn[layer], ffn_w_gu[i].astype(BF16), ffn_w_down[i].astype(BF16))
        else:
            proj = norm_matmul(h, norm_mix[layer], od_w_in[i].astype(BF16))
            h = conv_out(proj, od_conv_w[i], od_w_out[i].astype(BF16), h, batch, seq)
            final_gain = norm_final if layer == depth - 1 else None
            h = moe_block(h, norm_ffn[layer], moe_router[i], moe_w_gu[i], moe_w_down[i], final_gain)
    return h.reshape(batch, seq, d)
```

```python
import functools
import math

import jax
import jax.numpy as jnp
from jax import lax
from jax.experimental import pallas as pl
from jax.experimental.pallas import tpu as pltpu

EPS = 1e-6
CHUNK = 64
POOL_WINDOWS = (2, 4, 8, 16)
POOL_GROUP_DIM = 256
RET_HEADS = 4
RET_HEAD_DIM = 256
ROPE_BASE = 10000.0
N_EXPERTS = 8
TOP_K = 2

RET_BLOCK = 256
POOL_HALO = 128
CONV_HALO = 16
ROUTER_LANES = 128
GATHER_UNROLL = 8
LANES = 128
MOE_SUB_ROWS = 256

BF16 = jnp.bfloat16
F32 = jnp.float32
MIB = 1024 * 1024


def _params(semantics, vmem_mib):
    return pltpu.CompilerParams(dimension_semantics=semantics, vmem_limit_bytes=vmem_mib * MIB)


def _rms(x, g):
    ms = jnp.mean(x * x, axis=-1, keepdims=True)
    return x * lax.rsqrt(ms + EPS) * g


def _silu(x):
    return x / (1.0 + jnp.exp(-x))


def _norm_matmul_kernel(x_ref, g_ref, w_ref, o_ref, xn_ref):
    @pl.when(pl.program_id(1) == 0)
    def _():
        xn_ref[...] = _rms(x_ref[...], g_ref[...]).astype(xn_ref.dtype)

    o_ref[...] = jnp.dot(xn_ref[...], w_ref[...], preferred_element_type=F32).astype(o_ref.dtype)


def norm_matmul(x, g, w, *, tm=1024, tn=1024):
    t, d = x.shape
    n = w.shape[1]
    tm, tn = min(tm, t), min(tn, n)
    return pl.pallas_call(
        _norm_matmul_kernel,
        grid=(t // tm, n // tn),
        in_specs=[
            pl.BlockSpec((tm, d), lambda i, j: (i, 0)),
            pl.BlockSpec((1, d), lambda i, j: (0, 0)),
            pl.BlockSpec((d, tn), lambda i, j: (0, j)),
        ],
        out_specs=pl.BlockSpec((tm, tn), lambda i, j: (i, j)),
        out_shape=jax.ShapeDtypeStruct((t, n), BF16),
        scratch_shapes=[pltpu.VMEM((tm, d), BF16)],
        compiler_params=_params(("arbitrary", "arbitrary"), 48),
        name="norm_matmul",
    )(x, g.reshape(1, d), w)


def _even_core_kernel(u_ref, uh_ref, q_ref, k_ref, v_ref, g_ref, cos_ref, sin_ref, dm_ref,
                      wp_ref, ps_ref, y_ref, st_ref, *, ts, log_g):
    s = pl.program_id(1)
    gd = POOL_GROUP_DIM
    pw = len(POOL_WINDOWS) * gd

    @pl.when(s == 0)
    def _():
        st_ref[...] = jnp.zeros_like(st_ref)

    row = lax.broadcasted_iota(jnp.int32, (ts, ts), 0)
    col = lax.broadcasted_iota(jnp.int32, (ts, ts), 1)
    dist = row - col
    hrow = lax.broadcasted_iota(jnp.int32, (ts, POOL_HALO), 0)
    hcol = lax.broadcasted_iota(jnp.int32, (ts, POOL_HALO), 1)
    hdist = hrow - hcol + POOL_HALO
    t_seq = s * ts + lax.broadcasted_iota(jnp.int32, (ts, 1), 0)
    uh = jnp.where(s > 0, uh_ref[...], jnp.zeros_like(uh_ref))
    for g, w in enumerate(POOL_WINDOWS):
        cs = slice(g * gd, (g + 1) * gd)
        ug = u_ref[:, cs]
        band = jnp.logical_and(dist >= 0, dist < w).astype(BF16)
        hband = (hdist < w).astype(BF16)
        wsum = (jnp.dot(band, ug, preferred_element_type=F32)
                + jnp.dot(hband, uh[:, cs], preferred_element_type=F32))
        count = jnp.minimum(t_seq + 1, w).astype(F32)
        p = (wsum / count - ug.astype(F32)).astype(BF16)
        yg = jnp.dot(p, wp_ref[g], preferred_element_type=F32) * ps_ref[:, cs]
        y_ref[:, cs] = yg.astype(y_ref.dtype)

    lb = RET_BLOCK
    hd2 = RET_HEAD_DIM // 2
    n_idx = lax.broadcasted_iota(jnp.int32, (lb, 1), 0).astype(F32)
    for hd in range(RET_HEADS):
        lg = log_g[hd]
        q_decay = jnp.exp(lg * (n_idx + 1.0))
        k_decay = jnp.exp(lg * (lb - 1.0 - n_idx))
        block_decay = math.exp(lg * lb)
        c0 = hd * RET_HEAD_DIM
        for r in range(ts // lb):
            rows = slice(r * lb, (r + 1) * lb)
            cs_, sn_ = cos_ref[rows, :], sin_ref[rows, :]

            def rope(ref):
                x1 = ref[rows, c0:c0 + hd2].astype(F32)
                x2 = ref[rows, c0 + hd2:c0 + 2 * hd2].astype(F32)
                return jnp.concatenate([x1 * cs_ - x2 * sn_, x2 * cs_ + x1 * sn_], axis=-1)

            q = rope(q_ref) * (RET_HEAD_DIM ** -0.5)
            k = rope(k_ref)
            v = v_ref[rows, c0:c0 + RET_HEAD_DIM]
            scores = lax.dot_general(q.astype(BF16), k.astype(BF16), (((1,), (1,)), ((), ())),
                                     preferred_element_type=F32) * dm_ref[hd]
            state = st_ref[hd]
            o = (jnp.dot(scores.astype(BF16), v, preferred_element_type=F32)
                 + jnp.dot((q * q_decay).astype(BF16), state.astype(BF16), preferred_element_type=F32))
            st_ref[hd] = state * block_decay + lax.dot_general(
                (k * k_decay).astype(BF16), v, (((0,), (0,)), ((), ())), preferred_element_type=F32)
            mu = jnp.mean(o, axis=-1, keepdims=True)
            oc = o - mu
            var = jnp.mean(oc * oc, axis=-1, keepdims=True)
            gate = g_ref[rows, c0:c0 + RET_HEAD_DIM].astype(F32)
            y_ref[rows, pw + c0:pw + c0 + RET_HEAD_DIM] = (
                oc * lax.rsqrt(var + 1e-5) * _silu(gate)).astype(y_ref.dtype)


def even_core(proj, w_pool, pool_scale, batch, seq, *, ts=512):
    t = proj.shape[0]
    pw = len(POOL_WINDOWS) * POOL_GROUP_DIM
    rw = RET_HEADS * RET_HEAD_DIM
    assert pw == rw and proj.shape[1] == pw + 4 * rw
    ts = min(ts, seq)
    assert ts % RET_BLOCK == 0 and RET_BLOCK % CHUNK == 0 and ts % POOL_HALO == 0
    nsb = seq // ts
    half = RET_HEAD_DIM // 2

    pos = jnp.arange(seq, dtype=F32)
    inv = ROPE_BASE ** (-jnp.arange(half, dtype=F32) / half)
    ang = pos[:, None] * inv[None, :]
    cos, sin = jnp.cos(ang), jnp.sin(ang)
    log_g = tuple(math.log(1.0 - 2.0 ** (-5.0 - h)) for h in range(RET_HEADS))
    idx = jnp.arange(RET_BLOCK)
    visible = (idx[None, :] // CHUNK) <= (idx[:, None] // CHUNK)
    gap = jnp.abs(idx[:, None] - idx[None, :]).astype(F32)
    dmask = jnp.stack([jnp.where(visible, jnp.exp(lg * gap), 0.0) for lg in log_g]).astype(F32)

    def col(c):
        return pl.BlockSpec((ts, pw), lambda b, s: (b * nsb + s, c))

    halo = pl.BlockSpec(
        (POOL_HALO, pw), lambda b, s: (jnp.maximum((b * nsb + s) * (ts // POOL_HALO) - 1, 0), 0))
    return pl.pallas_call(
        functools.partial(_even_core_kernel, ts=ts, log_g=log_g),
        grid=(batch, nsb),
        in_specs=[
            col(0), halo, col(1), col(2), col(3), col(4),
            pl.BlockSpec((ts, half), lambda b, s: (s, 0)),
            pl.BlockSpec((ts, half), lambda b, s: (s, 0)),
            pl.BlockSpec((RET_HEADS, RET_BLOCK, RET_BLOCK), lambda b, s: (0, 0, 0)),
            pl.BlockSpec(w_pool.shape, lambda b, s: (0, 0, 0)),
            pl.BlockSpec((1, pw), lambda b, s: (0, 0)),
        ],
        out_specs=pl.BlockSpec((ts, pw + rw), lambda b, s: (b * nsb + s, 0)),
        out_shape=jax.ShapeDtypeStruct((t, pw + rw), BF16),
        scratch_shapes=[pltpu.VMEM((RET_HEADS, RET_HEAD_DIM, RET_HEAD_DIM), F32)],
        compiler_params=_params(("arbitrary", "arbitrary"), 48),
        name="even_core",
    )(proj, proj, proj, proj, proj, proj, cos, sin, dmask, w_pool, pool_scale.reshape(1, pw))


def _matmul_residual_kernel(y_ref, w_ref, h_ref, o_ref):
    o_ref[...] = h_ref[...] + jnp.dot(y_ref[...], w_ref[...], preferred_element_type=F32)


def matmul_residual(y, w, h, *, tm=512):
    t, k = y.shape
    d = w.shape[1]
    tm = min(tm, t)
    return pl.pallas_call(
        _matmul_residual_kernel,
        grid=(t // tm,),
        in_specs=[
            pl.BlockSpec((tm, k), lambda i: (i, 0)),
            pl.BlockSpec((k, d), lambda i: (0, 0)),
            pl.BlockSpec((tm, d), lambda i: (i, 0)),
        ],
        out_specs=pl.BlockSpec((tm, d), lambda i: (i, 0)),
        out_shape=jax.ShapeDtypeStruct((t, d), F32),
        compiler_params=_params(("arbitrary",), 48),
        name="matmul_residual",
    )(y, w, h)


def _ffn_kernel(x_ref, g_ref, wg_ref, wu_ref, wd_ref, o_ref, xn_ref):
    @pl.when(pl.program_id(1) == 0)
    def _():
        x = x_ref[...]
        xn_ref[...] = _rms(x, g_ref[...]).astype(xn_ref.dtype)
        o_ref[...] = x

    xn = xn_ref[...]
    gate = jnp.dot(xn, wg_ref[...], preferred_element_type=F32)
    up = jnp.dot(xn, wu_ref[...], preferred_element_type=F32)
    act = (_silu(gate) * up).astype(BF16)
    o_ref[...] += jnp.dot(act, wd_ref[...], preferred_element_type=F32)


def ffn(h, g, w_gu, w_down, *, tm=512, tf=512):
    t, d = h.shape
    f = w_down.shape[0]
    tm, tf = min(tm, t), min(tf, f)
    nj = f // tf
    return pl.pallas_call(
        _ffn_kernel,
        grid=(t // tm, nj),
        in_specs=[
            pl.BlockSpec((tm, d), lambda i, j: (i, 0)),
            pl.BlockSpec((1, d), lambda i, j: (0, 0)),
            pl.BlockSpec((d, tf), lambda i, j: (0, j)),
            pl.BlockSpec((d, tf), lambda i, j: (0, j + nj)),
            pl.BlockSpec((tf, d), lambda i, j: (j, 0)),
        ],
        out_specs=pl.BlockSpec((tm, d), lambda i, j: (i, 0)),
        out_shape=jax.ShapeDtypeStruct((t, d), F32),
        scratch_shapes=[pltpu.VMEM((tm, d), BF16)],
        compiler_params=_params(("arbitrary", "arbitrary"), 48),
        name="ffn",
    )(h, g.reshape(1, d), w_gu, w_gu, w_down)


def _conv_out_kernel(bg_ref, cg_ref, hx_ref, cgh_ref, hxh_ref, cw_ref, w_ref, h_ref, o_ref, *, ts):
    s = pl.program_id(1)
    z = cg_ref[...].astype(F32) * hx_ref[...].astype(F32)
    zh = cgh_ref[...].astype(F32) * hxh_ref[...].astype(F32)
    zh = jnp.where(s > 0, zh, 0.0)
    prev1 = zh[CONV_HALO - 1:CONV_HALO, :]
    prev2 = zh[CONV_HALO - 2:CONV_HALO - 1, :]
    row = lax.broadcasted_iota(jnp.int32, (ts, 1), 0)
    z1 = jnp.where(row == 0, prev1, pltpu.roll(z, 1, 0))
    z2 = jnp.where(row == 0, prev2, jnp.where(row == 1, prev1, pltpu.roll(z, 2, 0)))
    cw = cw_ref[...]
    conv = z2 * cw[0:1, :] + z1 * cw[1:2, :] + z * cw[2:3, :]
    act = (bg_ref[...].astype(F32) * conv).astype(BF16)
    o_ref[...] = h_ref[...] + jnp.dot(act, w_ref[...], preferred_element_type=F32)


def conv_out(proj, conv_w, w_out, h, batch, seq, *, ts=256):
    t, d = h.shape
    assert proj.shape[1] == 3 * d and conv_w.shape[0] == 3
    ts = min(ts, seq)
    nsb = seq // ts

    def col(c):
        return pl.BlockSpec((ts, d), lambda b, s: (b * nsb + s, c))

    def halo(c):
        return pl.BlockSpec(
            (CONV_HALO, d), lambda b, s: (jnp.maximum((b * nsb + s) * (ts // CONV_HALO) - 1, 0), c))

    return pl.pallas_call(
        functools.partial(_conv_out_kernel, ts=ts),
        grid=(batch, nsb),
        in_specs=[
            col(0), col(1), col(2), halo(1), halo(2),
            pl.BlockSpec(conv_w.shape, lambda b, s: (0, 0)),
            pl.BlockSpec(w_out.shape, lambda b, s: (0, 0)),
            pl.BlockSpec((ts, d), lambda b, s: (b * nsb + s, 0)),
        ],
        out_specs=pl.BlockSpec((ts, d), lambda b, s: (b * nsb + s, 0)),
        out_shape=jax.ShapeDtypeStruct((t, d), F32),
        compiler_params=_params(("arbitrary", "arbitrary"), 48),
        name="conv_out",
    )(proj, proj, proj, proj, proj, conv_w, w_out, h)


def _pack_bf16_pairs(x):
    n = x.shape[1] // 2
    hi = lax.bitcast_convert_type(x[:, :n].astype(BF16).astype(F32), jnp.uint32)
    lo = lax.bitcast_convert_type(x[:, n:].astype(BF16).astype(F32), jnp.uint32)
    return hi | (lo >> 16)


def _unpack_pairs_f32(w):
    hi = lax.bitcast_convert_type(w & jnp.uint32(0xFFFF0000), F32)
    lo = lax.bitcast_convert_type(w << 16, F32)
    return hi, lo


def _store_rows_as_tiles(ref, packed):
    rows = packed.shape[0]
    sub = packed.shape[1] // LANES
    for sl in range(sub):
        ref[pl.ds(sl, rows, stride=sub), :] = packed[:, sl * LANES:(sl + 1) * LANES]


def _load_tile_rows(ref, sl, rows, sub):
    return ref[pl.ds(sl, rows, stride=sub), :]


def _router_kernel(x_ref, g_ref, r_ref, xn_ref, idx_ref, wt_ref):
    xn = _rms(x_ref[...], g_ref[...])
    _store_rows_as_tiles(xn_ref, _pack_bf16_pairs(xn))
    logits = jnp.dot(xn, r_ref[...], preferred_element_type=F32, precision=lax.Precision.HIGHEST)
    lt = logits.T[:N_EXPERTS, :]
    e_id = lax.broadcasted_iota(jnp.int32, lt.shape, 0)
    m1 = jnp.max(lt, axis=0, keepdims=True)
    i1 = jnp.min(jnp.where(lt == m1, e_id, N_EXPERTS), axis=0, keepdims=True)
    rest = jnp.where(e_id == i1, -jnp.inf, lt)
    m2 = jnp.max(rest, axis=0, keepdims=True)
    i2 = jnp.min(jnp.where(rest == m2, e_id, N_EXPERTS), axis=0, keepdims=True)
    ex = jnp.exp(m2 - m1)
    w1 = 1.0 / (1.0 + ex)
    w2 = ex / (1.0 + ex)
    idx_ref[...] = jnp.where(e_id == 0, i1, jnp.where(e_id == 1, i2, 0))
    wt_ref[...] = jnp.where(e_id == 0, w1, jnp.where(e_id == 1, w2, 0.0))


def router(h, g, w_router, *, tm=512):
    t, d = h.shape
    e = w_router.shape[1]
    assert e == N_EXPERTS
    tm = min(tm, t)
    sub = d // 2 // LANES
    r_pad = jnp.zeros((d, ROUTER_LANES), F32).at[:, :e].set(w_router)
    return pl.pallas_call(
        _router_kernel,
        grid=(t // tm,),
        in_specs=[
            pl.BlockSpec((tm, d), lambda i: (i, 0)),
            pl.BlockSpec((1, d), lambda i: (0, 0)),
            pl.BlockSpec((d, ROUTER_LANES), lambda i: (0, 0)),
        ],
        out_specs=[
            pl.BlockSpec((tm * sub, LANES), lambda i: (i, 0)),
            pl.BlockSpec((e, tm), lambda i: (0, i)),
            pl.BlockSpec((e, tm), lambda i: (0, i)),
        ],
        out_shape=[
            jax.ShapeDtypeStruct((t * sub, LANES), jnp.uint32),
            jax.ShapeDtypeStruct((e, t), jnp.int32),
            jax.ShapeDtypeStruct((e, t), F32),
        ],
        compiler_params=_params(("arbitrary",), 48),
        name="router",
    )(h, g.reshape(1, d), r_pad)


def _routing_tables(idx, tm):
    k, t = idx.shape
    n_tiles = (k * t) // tm + N_EXPERTS
    flat = idx.reshape(-1)
    onehot = (flat[:, None] == jnp.arange(N_EXPERTS)[None, :]).astype(jnp.int32)
    csum = jnp.cumsum(onehot, axis=0)
    rank = jnp.sum(onehot * (csum - 1), axis=1)
    counts = csum[-1]
    padded = ((counts + tm - 1) // tm) * tm
    g_end = jnp.cumsum(padded)
    g_start = g_end - padded
    pos = jnp.sum(onehot * g_start[None, :], axis=1) + rank
    tok = jnp.tile(jnp.arange(t, dtype=jnp.int32), k)
    row_token = jnp.zeros((n_tiles * tm,), jnp.int32).at[pos].set(tok)
    n_active = (g_end[-1] // tm).astype(jnp.int32)
    tile_start = jnp.arange(n_tiles, dtype=jnp.int32) * tm
    tile_expert = jnp.sum((tile_start[:, None] >= g_end[None, :]).astype(jnp.int32), axis=1)
    tile_expert = jnp.minimum(tile_expert, N_EXPERTS - 1)
    tile_rows = jnp.clip((g_start + counts)[tile_expert] - tile_start, 0, tm)
    is_active = jnp.arange(n_tiles) < n_active
    tile_rows = jnp.where(is_active, tile_rows, 0).astype(jnp.int32)
    last = tile_expert[n_active - 1]
    tile_expert = jnp.where(is_active, tile_expert, last).astype(jnp.int32)
    return pos.reshape(k, t).astype(jnp.int32), row_token, tile_expert, tile_rows, n_active.reshape(1)


def _moe_kernel(te_ref, rt_ref, tr_ref, na_ref, x_hbm, wg_ref, wu_ref, wd_ref, o_ref, xbuf, xb, acc, sem,
                *, tm, nj):
    i = pl.program_id(0)
    j = pl.program_id(1)
    n_active = na_ref[0]
    n_rows = tr_ref[i]
    half = xb.shape[1] // 2
    sub = half // LANES

    def start_gather(tile, slot):
        base = tile * tm

        def body(r, carry):
            src = pl.multiple_of(rt_ref[base + r] * sub, sub)
            dst = pl.multiple_of(r * sub, sub)
            pltpu.make_async_copy(x_hbm.at[pl.ds(src, sub)], xbuf.at[slot, pl.ds(dst, sub)], sem.at[slot]).start()
            return carry

        lax.fori_loop(0, tm, body, 0, unroll=GATHER_UNROLL)

    @pl.when(j == 0)
    def _():
        @pl.when(i == 0)
        def _():
            start_gather(0, 0)

        @pl.when(n_rows > 0)
        def _():
            slot = i % 2
            pltpu.make_async_copy(x_hbm.at[pl.ds(0, tm * sub)], xbuf.at[slot], sem.at[slot]).wait()

            @pl.when(i + 1 < n_active)
            def _():
                start_gather(i + 1, (i + 1) % 2)

            for sl in range(sub):
                hi, lo = _unpack_pairs_f32(_load_tile_rows(xbuf.at[slot], sl, tm, sub))
                xb[:, sl * LANES:(sl + 1) * LANES] = hi.astype(BF16)
                xb[:, half + sl * LANES:half + (sl + 1) * LANES] = lo.astype(BF16)

        acc[...] = jnp.zeros_like(acc)

    def swiglu_rows(rows, wg, wu, wd):
        x = xb[rows, :]
        gate = jnp.dot(x, wg, preferred_element_type=F32)
        up = jnp.dot(x, wu, preferred_element_type=F32)
        act = (_silu(gate) * up).astype(BF16)
        acc[rows, :] += jnp.dot(act, wd, preferred_element_type=F32)

    @pl.when(n_rows == tm)
    def _():
        swiglu_rows(slice(None), wg_ref[...].astype(BF16), wu_ref[...].astype(BF16), wd_ref[...].astype(BF16))

    @pl.when(jnp.logical_and(n_rows > 0, n_rows < tm))
    def _():
        wg, wu, wd = wg_ref[...].astype(BF16), wu_ref[...].astype(BF16), wd_ref[...].astype(BF16)
        for sb in range(tm // MOE_SUB_ROWS):
            @pl.when(sb * MOE_SUB_ROWS < n_rows)
            def _():
                swiglu_rows(slice(sb * MOE_SUB_ROWS, (sb + 1) * MOE_SUB_ROWS), wg, wu, wd)

    @pl.when(j == nj - 1)
    def _():
        _store_rows_as_tiles(o_ref, _pack_bf16_pairs(acc[...]))


def moe_experts(xn, row_token, tile_expert, tile_rows, n_active, w_gu, w_down, layer, *, tm, tf=256):
    lanes = xn.shape[1]
    d = w_gu.shape[2]
    sub = d // 2 // lanes
    f = w_down.shape[2]
    tf = min(tf, f)
    nj = f // tf
    n_tiles = tile_expert.shape[0]
    assert tm % MOE_SUB_ROWS == 0

    def jj(i, j, na):
        return jnp.where(i < na[0], j, nj - 1)

    grid_spec = pltpu.PrefetchScalarGridSpec(
        num_scalar_prefetch=4,
        grid=(n_tiles, nj),
        in_specs=[
            pl.BlockSpec(memory_space=pl.ANY),
            pl.BlockSpec((None, None, d, tf), lambda i, j, te, rt, tr, na: (layer, te[i], 0, jj(i, j, na))),
            pl.BlockSpec((None, None, d, tf), lambda i, j, te, rt, tr, na: (layer, te[i], 0, jj(i, j, na) + nj)),
            pl.BlockSpec((None, None, tf, d), lambda i, j, te, rt, tr, na: (layer, te[i], jj(i, j, na), 0)),
        ],
        out_specs=pl.BlockSpec((tm * sub, lanes), lambda i, j, te, rt, tr, na: (i, 0)),
        scratch_shapes=[
            pltpu.VMEM((2, tm * sub, lanes), jnp.uint32),
            pltpu.VMEM((tm, d), BF16),
            pltpu.VMEM((tm, d), F32),
            pltpu.SemaphoreType.DMA((2,)),
        ],
    )
    return pl.pallas_call(
        functools.partial(_moe_kernel, tm=tm, nj=nj),
        grid_spec=grid_spec,
        out_shape=jax.ShapeDtypeStruct((n_tiles * tm * sub, lanes), jnp.uint32),
        compiler_params=_params(("arbitrary", "arbitrary"), 56),
        name="moe_experts",
    )(tile_expert, row_token, tile_rows, n_active, xn, w_gu, w_gu, w_down)


def _combine_kernel(p0_ref, p1_ref, ys_hbm, h_ref, wt_ref, gf_ref, o_ref, buf, sem, *, tc, final_norm):
    i = pl.program_id(0)
    n = pl.num_programs(0)
    half = o_ref.shape[1] // 2
    sub = half // LANES

    def start_gather(tile, slot):
        base = tile * tc

        def body(r, carry):
            dst = pl.ds(pl.multiple_of(r * sub, sub), sub)
            for kk, p_ref in enumerate((p0_ref, p1_ref)):
                src = pl.ds(pl.multiple_of(p_ref[base + r] * sub, sub), sub)
                pltpu.make_async_copy(ys_hbm.at[src], buf.at[slot, kk, dst], sem.at[slot]).start()
            return carry

        lax.fori_loop(0, tc, body, 0, unroll=GATHER_UNROLL)

    @pl.when(i == 0)
    def _():
        start_gather(0, 0)

    slot = i % 2
    for kk in range(TOP_K):
        pltpu.make_async_copy(ys_hbm.at[pl.ds(0, tc * sub)], buf.at[slot, kk], sem.at[slot]).wait()

    @pl.when(i + 1 < n)
    def _():
        start_gather(i + 1, (i + 1) % 2)

    wt = wt_ref[...].T
    w0, w1 = wt[:, 0:1], wt[:, 1:2]
    for sl in range(sub):
        a_hi, a_lo = _unpack_pairs_f32(_load_tile_rows(buf.at[slot, 0], sl, tc, sub))
        b_hi, b_lo = _unpack_pairs_f32(_load_tile_rows(buf.at[slot, 1], sl, tc, sub))
        c_hi = slice(sl * LANES, (sl + 1) * LANES)
        c_lo = slice(half + sl * LANES, half + (sl + 1) * LANES)
        o_ref[:, c_hi] = h_ref[:, c_hi] + (w0 * a_hi + w1 * b_hi)
        o_ref[:, c_lo] = h_ref[:, c_lo] + (w0 * a_lo + w1 * b_lo)
    if final_norm:
        o_ref[...] = _rms(o_ref[...], gf_ref[...])


def moe_combine(ys, pos, wts, h, final_gain, *, tc=256):
    t, d = h.shape
    lanes = ys.shape[1]
    sub = d // 2 // lanes
    tc = min(tc, t)
    final_norm = final_gain is not None
    gf = (final_gain if final_norm else jnp.ones((d,), F32)).reshape(1, d)
    grid_spec = pltpu.PrefetchScalarGridSpec(
        num_scalar_prefetch=2,
        grid=(t // tc,),
        in_specs=[
            pl.BlockSpec(memory_space=pl.ANY),
            pl.BlockSpec((tc, d), lambda i, p0, p1: (i, 0)),
            pl.BlockSpec((N_EXPERTS, tc), lambda i, p0, p1: (0, i)),
            pl.BlockSpec((1, d), lambda i, p0, p1: (0, 0)),
        ],
        out_specs=pl.BlockSpec((tc, d), lambda i, p0, p1: (i, 0)),
        scratch_shapes=[
            pltpu.VMEM((2, TOP_K, tc * sub, lanes), jnp.uint32),
            pltpu.SemaphoreType.DMA((2,)),
        ],
    )
    return pl.pallas_call(
        functools.partial(_combine_kernel, tc=tc, final_norm=final_norm),
        grid_spec=grid_spec,
        out_shape=jax.ShapeDtypeStruct((t, d), F32),
        compiler_params=_params(("arbitrary",), 48),
        name="moe_combine",
    )(pos[0], pos[1], ys, h, wts, gf)


def moe_block(h, g, w_router, w_gu, w_down, layer, final_gain, *, tm=1024):
    t = h.shape[0]
    tm = min(tm, t)
    xn, idx, wts = router(h, g, w_router)
    pos, row_token, tile_expert, tile_rows, n_active = _routing_tables(idx[:TOP_K], tm)
    ys = moe_experts(xn, row_token, tile_expert, tile_rows, n_active, w_gu, w_down, layer, tm=tm)
    return moe_combine(ys, pos, wts, h, final_gain)


def kernel(x, norm_mix, norm_ffn, norm_final, ev_w_in, ev_pool_w, ev_pool_scale, ev_w_out,
           od_w_in, od_conv_w, od_w_out, ffn_w_gu, ffn_w_down, moe_router, moe_w_gu, moe_w_down):
    batch, seq, d = x.shape
    depth = norm_mix.shape[0]
    h = x.reshape(batch * seq, d)
    for layer in range(depth):
        i = layer // 2
        if layer % 2 == 0:
            proj = norm_matmul(h, norm_mix[layer], ev_w_in[i].astype(BF16))
            y = even_core(proj, ev_pool_w[i].astype(BF16), ev_pool_scale[i], batch, seq)
            h = matmul_residual(y, ev_w_out[i].astype(BF16), h)
            h = ffn(h, norm_ffn[layer], ffn_w_gu[i].astype(BF16), ffn_w_down[i].astype(BF16))
        else:
            proj = norm_matmul(h, norm_mix[layer], od_w_in[i].astype(BF16))
            h = conv_out(proj, od_conv_w[i], od_w_out[i].astype(BF16), h, batch, seq)
            final_gain = norm_final if layer == depth - 1 else None
            h = moe_block(h, norm_ffn[layer], moe_router[i], moe_w_gu, moe_w_down, i, final_gain)
    return h.reshape(batch, seq, d)
```

```python
import functools
import math

import jax
import jax.numpy as jnp
from jax import lax
from jax.experimental import pallas as pl
from jax.experimental.pallas import tpu as pltpu

EPS = 1e-6
CHUNK = 64
POOL_WINDOWS = (2, 4, 8, 16)
POOL_GROUP_DIM = 256
RET_HEADS = 4
RET_HEAD_DIM = 256
ROPE_BASE = 10000.0
N_EXPERTS = 8
TOP_K = 2

RET_BLOCK = 256
POOL_HALO = 128
CONV_HALO = 16
ROUTER_LANES = 128
GATHER_UNROLL = 8
LANES = 128
MOE_SUB_ROWS = 256

BF16 = jnp.bfloat16
F32 = jnp.float32
MIB = 1024 * 1024


def _params(semantics, vmem_mib):
    return pltpu.CompilerParams(dimension_semantics=semantics, vmem_limit_bytes=vmem_mib * MIB)


def _rms(x, g):
    ms = jnp.mean(x * x, axis=-1, keepdims=True)
    return x * lax.rsqrt(ms + EPS) * g


def _silu(x):
    return x / (1.0 + jnp.exp(-x))


def _norm_matmul_kernel(x_ref, g_ref, w_ref, o_ref, xn_ref):
    @pl.when(pl.program_id(1) == 0)
    def _():
        xn_ref[...] = _rms(x_ref[...], g_ref[...]).astype(xn_ref.dtype)

    w = w_ref[...].astype(BF16)
    o_ref[...] = jnp.dot(xn_ref[...], w, preferred_element_type=F32).astype(o_ref.dtype)


def norm_matmul(x, g, w, layer, *, tm=1024, tn=1024):
    t, d = x.shape
    n = w.shape[2]
    tm, tn = min(tm, t), min(tn, n)
    return pl.pallas_call(
        _norm_matmul_kernel,
        grid=(t // tm, n // tn),
        in_specs=[
            pl.BlockSpec((tm, d), lambda i, j: (i, 0)),
            pl.BlockSpec((1, d), lambda i, j: (0, 0)),
            pl.BlockSpec((None, d, tn), lambda i, j: (layer, 0, j)),
        ],
        out_specs=pl.BlockSpec((tm, tn), lambda i, j: (i, j)),
        out_shape=jax.ShapeDtypeStruct((t, n), BF16),
        scratch_shapes=[pltpu.VMEM((tm, d), BF16)],
        compiler_params=_params(("arbitrary", "arbitrary"), 48),
        name="norm_matmul",
    )(x, g.reshape(1, d), w)


def _even_core_kernel(u_ref, uh_ref, q_ref, k_ref, v_ref, g_ref, cos_ref, sin_ref, dm_ref,
                      wp_ref, ps_ref, y_ref, st_ref, *, ts, log_g):
    s = pl.program_id(1)
    gd = POOL_GROUP_DIM
    pw = len(POOL_WINDOWS) * gd

    @pl.when(s == 0)
    def _():
        st_ref[...] = jnp.zeros_like(st_ref)

    row = lax.broadcasted_iota(jnp.int32, (ts, ts), 0)
    col = lax.broadcasted_iota(jnp.int32, (ts, ts), 1)
    dist = row - col
    hrow = lax.broadcasted_iota(jnp.int32, (ts, POOL_HALO), 0)
    hcol = lax.broadcasted_iota(jnp.int32, (ts, POOL_HALO), 1)
    hdist = hrow - hcol + POOL_HALO
    t_seq = s * ts + lax.broadcasted_iota(jnp.int32, (ts, 1), 0)
    uh = jnp.where(s > 0, uh_ref[...], jnp.zeros_like(uh_ref))
    for g, w in enumerate(POOL_WINDOWS):
        cs = slice(g * gd, (g + 1) * gd)
        ug = u_ref[:, cs]
        band = jnp.logical_and(dist >= 0, dist < w).astype(BF16)
        hband = (hdist < w).astype(BF16)
        wsum = (jnp.dot(band, ug, preferred_element_type=F32)
                + jnp.dot(hband, uh[:, cs], preferred_element_type=F32))
        count = jnp.minimum(t_seq + 1, w).astype(F32)
        p = (wsum / count - ug.astype(F32)).astype(BF16)
        yg = jnp.dot(p, wp_ref[g], preferred_element_type=F32) * ps_ref[:, cs]
        y_ref[:, cs] = yg.astype(y_ref.dtype)

    lb = RET_BLOCK
    hd2 = RET_HEAD_DIM // 2
    n_idx = lax.broadcasted_iota(jnp.int32, (lb, 1), 0).astype(F32)
    for hd in range(RET_HEADS):
        lg = log_g[hd]
        q_decay = jnp.exp(lg * (n_idx + 1.0))
        k_decay = jnp.exp(lg * (lb - 1.0 - n_idx))
        block_decay = math.exp(lg * lb)
        c0 = hd * RET_HEAD_DIM
        for r in range(ts // lb):
            rows = slice(r * lb, (r + 1) * lb)
            cs_, sn_ = cos_ref[rows, :], sin_ref[rows, :]

            def rope(ref):
                x1 = ref[rows, c0:c0 + hd2].astype(F32)
                x2 = ref[rows, c0 + hd2:c0 + 2 * hd2].astype(F32)
                return jnp.concatenate([x1 * cs_ - x2 * sn_, x2 * cs_ + x1 * sn_], axis=-1)

            q = rope(q_ref) * (RET_HEAD_DIM ** -0.5)
            k = rope(k_ref)
            v = v_ref[rows, c0:c0 + RET_HEAD_DIM]
            scores = lax.dot_general(q.astype(BF16), k.astype(BF16), (((1,), (1,)), ((), ())),
                                     preferred_element_type=F32) * dm_ref[hd]
            state = st_ref[hd]
            o = (jnp.dot(scores.astype(BF16), v, preferred_element_type=F32)
                 + jnp.dot((q * q_decay).astype(BF16), state.astype(BF16), preferred_element_type=F32))
            st_ref[hd] = state * block_decay + lax.dot_general(
                (k * k_decay).astype(BF16), v, (((0,), (0,)), ((), ())), preferred_element_type=F32)
            mu = jnp.mean(o, axis=-1, keepdims=True)
            oc = o - mu
            var = jnp.mean(oc * oc, axis=-1, keepdims=True)
            gate = g_ref[rows, c0:c0 + RET_HEAD_DIM].astype(F32)
            y_ref[rows, pw + c0:pw + c0 + RET_HEAD_DIM] = (
                oc * lax.rsqrt(var + 1e-5) * _silu(gate)).astype(y_ref.dtype)


def even_core(proj, w_pool, pool_scale, batch, seq, *, ts=512):
    t = proj.shape[0]
    pw = len(POOL_WINDOWS) * POOL_GROUP_DIM
    rw = RET_HEADS * RET_HEAD_DIM
    assert pw == rw and proj.shape[1] == pw + 4 * rw
    ts = min(ts, seq)
    assert ts % RET_BLOCK == 0 and RET_BLOCK % CHUNK == 0 and ts % POOL_HALO == 0
    nsb = seq // ts
    half = RET_HEAD_DIM // 2

    pos = jnp.arange(seq, dtype=F32)
    inv = ROPE_BASE ** (-jnp.arange(half, dtype=F32) / half)
    ang = pos[:, None] * inv[None, :]
    cos, sin = jnp.cos(ang), jnp.sin(ang)
    log_g = tuple(math.log(1.0 - 2.0 ** (-5.0 - h)) for h in range(RET_HEADS))
    idx = jnp.arange(RET_BLOCK)
    visible = (idx[None, :] // CHUNK) <= (idx[:, None] // CHUNK)
    gap = jnp.abs(idx[:, None] - idx[None, :]).astype(F32)
    dmask = jnp.stack([jnp.where(visible, jnp.exp(lg * gap), 0.0) for lg in log_g]).astype(F32)

    def col(c):
        return pl.BlockSpec((ts, pw), lambda b, s: (b * nsb + s, c))

    halo = pl.BlockSpec(
        (POOL_HALO, pw), lambda b, s: (jnp.maximum((b * nsb + s) * (ts // POOL_HALO) - 1, 0), 0))
    return pl.pallas_call(
        functools.partial(_even_core_kernel, ts=ts, log_g=log_g),
        grid=(batch, nsb),
        in_specs=[
            col(0), halo, col(1), col(2), col(3), col(4),
            pl.BlockSpec((ts, half), lambda b, s: (s, 0)),
            pl.BlockSpec((ts, half), lambda b, s: (s, 0)),
            pl.BlockSpec((RET_HEADS, RET_BLOCK, RET_BLOCK), lambda b, s: (0, 0, 0)),
            pl.BlockSpec(w_pool.shape, lambda b, s: (0, 0, 0)),
            pl.BlockSpec((1, pw), lambda b, s: (0, 0)),
        ],
        out_specs=pl.BlockSpec((ts, pw + rw), lambda b, s: (b * nsb + s, 0)),
        out_shape=jax.ShapeDtypeStruct((t, pw + rw), BF16),
        scratch_shapes=[pltpu.VMEM((RET_HEADS, RET_HEAD_DIM, RET_HEAD_DIM), F32)],
        compiler_params=_params(("arbitrary", "arbitrary"), 48),
        name="even_core",
    )(proj, proj, proj, proj, proj, proj, cos, sin, dmask, w_pool, pool_scale.reshape(1, pw))


def _matmul_residual_kernel(y_ref, w_ref, h_ref, o_ref):
    o_ref[...] = h_ref[...] + jnp.dot(y_ref[...], w_ref[...], preferred_element_type=F32)


def matmul_residual(y, w, h, *, tm=512):
    t, k = y.shape
    d = w.shape[1]
    tm = min(tm, t)
    return pl.pallas_call(
        _matmul_residual_kernel,
        grid=(t // tm,),
        in_specs=[
            pl.BlockSpec((tm, k), lambda i: (i, 0)),
            pl.BlockSpec((k, d), lambda i: (0, 0)),
            pl.BlockSpec((tm, d), lambda i: (i, 0)),
        ],
        out_specs=pl.BlockSpec((tm, d), lambda i: (i, 0)),
        out_shape=jax.ShapeDtypeStruct((t, d), F32),
        compiler_params=_params(("arbitrary",), 48),
        name="matmul_residual",
    )(y, w, h)


def _ffn_kernel(x_ref, g_ref, wg_ref, wu_ref, wd_ref, o_ref, xn_ref):
    @pl.when(pl.program_id(1) == 0)
    def _():
        x = x_ref[...]
        xn_ref[...] = _rms(x, g_ref[...]).astype(xn_ref.dtype)
        o_ref[...] = x

    xn = xn_ref[...]
    gate = jnp.dot(xn, wg_ref[...], preferred_element_type=F32)
    up = jnp.dot(xn, wu_ref[...], preferred_element_type=F32)
    act = (_silu(gate) * up).astype(BF16)
    o_ref[...] += jnp.dot(act, wd_ref[...], preferred_element_type=F32)


def ffn(h, g, w_gu, w_down, *, tm=512, tf=512):
    t, d = h.shape
    f = w_down.shape[0]
    tm, tf = min(tm, t), min(tf, f)
    nj = f // tf
    return pl.pallas_call(
        _ffn_kernel,
        grid=(t // tm, nj),
        in_specs=[
            pl.BlockSpec((tm, d), lambda i, j: (i, 0)),
            pl.BlockSpec((1, d), lambda i, j: (0, 0)),
            pl.BlockSpec((d, tf), lambda i, j: (0, j)),
            pl.BlockSpec((d, tf), lambda i, j: (0, j + nj)),
            pl.BlockSpec((tf, d), lambda i, j: (j, 0)),
        ],
        out_specs=pl.BlockSpec((tm, d), lambda i, j: (i, 0)),
        out_shape=jax.ShapeDtypeStruct((t, d), F32),
        scratch_shapes=[pltpu.VMEM((tm, d), BF16)],
        compiler_params=_params(("arbitrary", "arbitrary"), 48),
        name="ffn",
    )(h, g.reshape(1, d), w_gu, w_gu, w_down)


def _conv_out_kernel(bg_ref, cg_ref, hx_ref, cgh_ref, hxh_ref, cw_ref, w_ref, h_ref, o_ref, *, ts):
    s = pl.program_id(1)
    z = cg_ref[...].astype(F32) * hx_ref[...].astype(F32)
    zh = cgh_ref[...].astype(F32) * hxh_ref[...].astype(F32)
    zh = jnp.where(s > 0, zh, 0.0)
    prev1 = zh[CONV_HALO - 1:CONV_HALO, :]
    prev2 = zh[CONV_HALO - 2:CONV_HALO - 1, :]
    row = lax.broadcasted_iota(jnp.int32, (ts, 1), 0)
    z1 = jnp.where(row == 0, prev1, pltpu.roll(z, 1, 0))
    z2 = jnp.where(row == 0, prev2, jnp.where(row == 1, prev1, pltpu.roll(z, 2, 0)))
    cw = cw_ref[...]
    conv = z2 * cw[0:1, :] + z1 * cw[1:2, :] + z * cw[2:3, :]
    act = (bg_ref[...].astype(F32) * conv).astype(BF16)
    o_ref[...] = h_ref[...] + jnp.dot(act, w_ref[...], preferred_element_type=F32)


def conv_out(proj, conv_w, w_out, h, batch, seq, *, ts=256):
    t, d = h.shape
    assert proj.shape[1] == 3 * d and conv_w.shape[0] == 3
    ts = min(ts, seq)
    nsb = seq // ts

    def col(c):
        return pl.BlockSpec((ts, d), lambda b, s: (b * nsb + s, c))

    def halo(c):
        return pl.BlockSpec(
            (CONV_HALO, d), lambda b, s: (jnp.maximum((b * nsb + s) * (ts // CONV_HALO) - 1, 0), c))

    return pl.pallas_call(
        functools.partial(_conv_out_kernel, ts=ts),
        grid=(batch, nsb),
        in_specs=[
            col(0), col(1), col(2), halo(1), halo(2),
            pl.BlockSpec(conv_w.shape, lambda b, s: (0, 0)),
            pl.BlockSpec(w_out.shape, lambda b, s: (0, 0)),
            pl.BlockSpec((ts, d), lambda b, s: (b * nsb + s, 0)),
        ],
        out_specs=pl.BlockSpec((ts, d), lambda b, s: (b * nsb + s, 0)),
        out_shape=jax.ShapeDtypeStruct((t, d), F32),
        compiler_params=_params(("arbitrary", "arbitrary"), 48),
        name="conv_out",
    )(proj, proj, proj, proj, proj, conv_w, w_out, h)


def _pack_bf16_pairs(x):
    n = x.shape[1] // 2
    hi = lax.bitcast_convert_type(x[:, :n].astype(BF16).astype(F32), jnp.uint32)
    lo = lax.bitcast_convert_type(x[:, n:].astype(BF16).astype(F32), jnp.uint32)
    return hi | (lo >> 16)


def _unpack_pairs_f32(w):
    hi = lax.bitcast_convert_type(w & jnp.uint32(0xFFFF0000), F32)
    lo = lax.bitcast_convert_type(w << 16, F32)
    return hi, lo


def _store_rows_as_tiles(ref, packed):
    rows = packed.shape[0]
    sub = packed.shape[1] // LANES
    for sl in range(sub):
        ref[pl.ds(sl, rows, stride=sub), :] = packed[:, sl * LANES:(sl + 1) * LANES]


def _load_tile_rows(ref, sl, rows, sub):
    return ref[pl.ds(sl, rows, stride=sub), :]


def _router_kernel(x_ref, g_ref, r_ref, xn_ref, idx_ref, wt_ref):
    xn = _rms(x_ref[...], g_ref[...])
    _store_rows_as_tiles(xn_ref, _pack_bf16_pairs(xn))
    logits = jnp.dot(xn, r_ref[...], preferred_element_type=F32, precision=lax.Precision.HIGHEST)
    lt = logits.T[:N_EXPERTS, :]
    e_id = lax.broadcasted_iota(jnp.int32, lt.shape, 0)
    m1 = jnp.max(lt, axis=0, keepdims=True)
    i1 = jnp.min(jnp.where(lt == m1, e_id, N_EXPERTS), axis=0, keepdims=True)
    rest = jnp.where(e_id == i1, -jnp.inf, lt)
    m2 = jnp.max(rest, axis=0, keepdims=True)
    i2 = jnp.min(jnp.where(rest == m2, e_id, N_EXPERTS), axis=0, keepdims=True)
    ex = jnp.exp(m2 - m1)
    w1 = 1.0 / (1.0 + ex)
    w2 = ex / (1.0 + ex)
    idx_ref[...] = jnp.where(e_id == 0, i1, jnp.where(e_id == 1, i2, 0))
    wt_ref[...] = jnp.where(e_id == 0, w1, jnp.where(e_id == 1, w2, 0.0))


def router(h, g, w_router, *, tm=512):
    t, d = h.shape
    e = w_router.shape[1]
    assert e == N_EXPERTS
    tm = min(tm, t)
    sub = d // 2 // LANES
    r_pad = jnp.zeros((d, ROUTER_LANES), F32).at[:, :e].set(w_router)
    return pl.pallas_call(
        _router_kernel,
        grid=(t // tm,),
        in_specs=[
            pl.BlockSpec((tm, d), lambda i: (i, 0)),
            pl.BlockSpec((1, d), lambda i: (0, 0)),
            pl.BlockSpec((d, ROUTER_LANES), lambda i: (0, 0)),
        ],
        out_specs=[
            pl.BlockSpec((tm * sub, LANES), lambda i: (i, 0)),
            pl.BlockSpec((e, tm), lambda i: (0, i)),
            pl.BlockSpec((e, tm), lambda i: (0, i)),
        ],
        out_shape=[
            jax.ShapeDtypeStruct((t * sub, LANES), jnp.uint32),
            jax.ShapeDtypeStruct((e, t), jnp.int32),
            jax.ShapeDtypeStruct((e, t), F32),
        ],
        compiler_params=_params(("arbitrary",), 48),
        name="router",
    )(h, g.reshape(1, d), r_pad)


def _routing_tables(idx, tm):
    k, t = idx.shape
    n_tiles = (k * t) // tm + N_EXPERTS
    flat = idx.reshape(-1)
    onehot = (flat[:, None] == jnp.arange(N_EXPERTS)[None, :]).astype(jnp.int32)
    csum = jnp.cumsum(onehot, axis=0)
    rank = jnp.sum(onehot * (csum - 1), axis=1)
    counts = csum[-1]
    padded = ((counts + tm - 1) // tm) * tm
    g_end = jnp.cumsum(padded)
    g_start = g_end - padded
    pos = jnp.sum(onehot * g_start[None, :], axis=1) + rank
    n_active = (g_end[-1] // tm).astype(jnp.int32)
    tile_start = jnp.arange(n_tiles, dtype=jnp.int32) * tm
    tile_expert = jnp.sum((tile_start[:, None] >= g_end[None, :]).astype(jnp.int32), axis=1)
    tile_expert = jnp.minimum(tile_expert, N_EXPERTS - 1)
    tile_rows = jnp.clip((g_start + counts)[tile_expert] - tile_start, 0, tm)
    is_active = jnp.arange(n_tiles) < n_active
    tile_rows = jnp.where(is_active, tile_rows, 0).astype(jnp.int32)
    last = tile_expert[n_active - 1]
    tile_expert = jnp.where(is_active, tile_expert, last).astype(jnp.int32)
    return pos.reshape(k, t).astype(jnp.int32), tile_expert, tile_rows, n_active.reshape(1)


def _dispatch_kernel(p0_ref, p1_ref, x_ref, zeros_hbm, o_hbm, sem, *, tt, sub):
    del zeros_hbm
    base = pl.program_id(0) * tt

    def body(r, carry):
        src = x_ref.at[pl.ds(pl.multiple_of(r * sub, sub), sub)]
        for p_ref in (p0_ref, p1_ref):
            dst = pl.ds(pl.multiple_of(p_ref[base + r] * sub, sub), sub)
            pltpu.make_async_copy(src, o_hbm.at[dst], sem).start()
        return carry

    lax.fori_loop(0, tt, body, 0, unroll=GATHER_UNROLL)
    for _ in range(TOP_K):
        pltpu.make_async_copy(x_ref, o_hbm.at[pl.ds(0, tt * sub)], sem).wait()


def moe_dispatch(xn, pos, n_rows, *, tt=1024):
    lanes = xn.shape[1]
    t = pos.shape[1]
    sub = xn.shape[0] // t
    tt = min(tt, t)
    grid_spec = pltpu.PrefetchScalarGridSpec(
        num_scalar_prefetch=2,
        grid=(t // tt,),
        in_specs=[
            pl.BlockSpec((tt * sub, lanes), lambda i, p0, p1: (i, 0)),
            pl.BlockSpec(memory_space=pl.ANY),
        ],
        out_specs=pl.BlockSpec(memory_space=pl.ANY),
        scratch_shapes=[pltpu.SemaphoreType.DMA(())],
    )
    return pl.pallas_call(
        functools.partial(_dispatch_kernel, tt=tt, sub=sub),
        grid_spec=grid_spec,
        out_shape=jax.ShapeDtypeStruct((n_rows * sub, lanes), jnp.uint32),
        input_output_aliases={3: 0},
        compiler_params=_params(("arbitrary",), 32),
        name="moe_dispatch",
    )(pos[0], pos[1], xn, jnp.zeros((n_rows * sub, lanes), jnp.uint32))


def _moe_kernel(te_ref, tr_ref, na_ref, x_ref, wg_ref, wu_ref, wd_ref, o_ref, xb, acc, *, tm, nj):
    j = pl.program_id(1)
    n_rows = tr_ref[pl.program_id(0)]
    half = xb.shape[1] // 2
    sub = half // LANES

    @pl.when(j == 0)
    def _():
        @pl.when(n_rows > 0)
        def _():
            for sl in range(sub):
                hi, lo = _unpack_pairs_f32(_load_tile_rows(x_ref, sl, tm, sub))
                xb[:, sl * LANES:(sl + 1) * LANES] = hi.astype(BF16)
                xb[:, half + sl * LANES:half + (sl + 1) * LANES] = lo.astype(BF16)

        acc[...] = jnp.zeros_like(acc)

    def swiglu_rows(rows, wg, wu, wd):
        x = xb[rows, :]
        gate = jnp.dot(x, wg, preferred_element_type=F32)
        up = jnp.dot(x, wu, preferred_element_type=F32)
        act = (_silu(gate) * up).astype(BF16)
        acc[rows, :] += jnp.dot(act, wd, preferred_element_type=F32)

    @pl.when(n_rows == tm)
    def _():
        swiglu_rows(slice(None), wg_ref[...].astype(BF16), wu_ref[...].astype(BF16), wd_ref[...].astype(BF16))

    @pl.when(jnp.logical_and(n_rows > 0, n_rows < tm))
    def _():
        wg, wu, wd = wg_ref[...].astype(BF16), wu_ref[...].astype(BF16), wd_ref[...].astype(BF16)
        for sb in range(tm // MOE_SUB_ROWS):
            @pl.when(sb * MOE_SUB_ROWS < n_rows)
            def _():
                swiglu_rows(slice(sb * MOE_SUB_ROWS, (sb + 1) * MOE_SUB_ROWS), wg, wu, wd)

    @pl.when(j == nj - 1)
    def _():
        _store_rows_as_tiles(o_ref, _pack_bf16_pairs(acc[...]))


def moe_experts(xs, tile_expert, tile_rows, n_active, w_gu, w_down, layer, *, tm, tf=256):
    lanes = xs.shape[1]
    d = w_gu.shape[2]
    sub = d // 2 // lanes
    f = w_down.shape[2]
    tf = min(tf, f)
    nj = f // tf
    n_tiles = tile_expert.shape[0]
    assert tm % MOE_SUB_ROWS == 0 and xs.shape[0] == n_tiles * tm * sub

    def jj(i, j, na):
        return jnp.where(i < na[0], j, nj - 1)

    def ii(i, na):
        return jnp.minimum(i, na[0] - 1)

    grid_spec = pltpu.PrefetchScalarGridSpec(
        num_scalar_prefetch=3,
        grid=(n_tiles, nj),
        in_specs=[
            pl.BlockSpec((tm * sub, lanes), lambda i, j, te, tr, na: (ii(i, na), 0)),
            pl.BlockSpec((None, None, d, tf), lambda i, j, te, tr, na: (layer, te[i], 0, jj(i, j, na))),
            pl.BlockSpec((None, None, d, tf), lambda i, j, te, tr, na: (layer, te[i], 0, jj(i, j, na) + nj)),
            pl.BlockSpec((None, None, tf, d), lambda i, j, te, tr, na: (layer, te[i], jj(i, j, na), 0)),
        ],
        out_specs=pl.BlockSpec((tm * sub, lanes), lambda i, j, te, tr, na: (i, 0)),
        scratch_shapes=[
            pltpu.VMEM((tm, d), BF16),
            pltpu.VMEM((tm, d), F32),
        ],
    )
    return pl.pallas_call(
        functools.partial(_moe_kernel, tm=tm, nj=nj),
        grid_spec=grid_spec,
        out_shape=jax.ShapeDtypeStruct((n_tiles * tm * sub, lanes), jnp.uint32),
        compiler_params=_params(("arbitrary", "arbitrary"), 56),
        name="moe_experts",
    )(tile_expert, tile_rows, n_active, xs, w_gu, w_gu, w_down)


def _combine_kernel(p0_ref, p1_ref, ys_hbm, h_ref, wt_ref, gf_ref, o_ref, buf, sem, *, tc, final_norm):
    i = pl.program_id(0)
    n = pl.num_programs(0)
    half = o_ref.shape[1] // 2
    sub = half // LANES

    def start_gather(tile, slot):
        base = tile * tc

        def body(r, carry):
            dst = pl.ds(pl.multiple_of(r * sub, sub), sub)
            for kk, p_ref in enumerate((p0_ref, p1_ref)):
                src = pl.ds(pl.multiple_of(p_ref[base + r] * sub, sub), sub)
                pltpu.make_async_copy(ys_hbm.at[src], buf.at[slot, kk, dst], sem.at[slot]).start()
            return carry

        lax.fori_loop(0, tc, body, 0, unroll=GATHER_UNROLL)

    @pl.when(i == 0)
    def _():
        start_gather(0, 0)

    slot = i % 2
    for kk in range(TOP_K):
        pltpu.make_async_copy(ys_hbm.at[pl.ds(0, tc * sub)], buf.at[slot, kk], sem.at[slot]).wait()

    @pl.when(i + 1 < n)
    def _():
        start_gather(i + 1, (i + 1) % 2)

    wt = wt_ref[...].T
    w0, w1 = wt[:, 0:1], wt[:, 1:2]
    for sl in range(sub):
        a_hi, a_lo = _unpack_pairs_f32(_load_tile_rows(buf.at[slot, 0], sl, tc, sub))
        b_hi, b_lo = _unpack_pairs_f32(_load_tile_rows(buf.at[slot, 1], sl, tc, sub))
        c_hi = slice(sl * LANES, (sl + 1) * LANES)
        c_lo = slice(half + sl * LANES, half + (sl + 1) * LANES)
        o_ref[:, c_hi] = h_ref[:, c_hi] + (w0 * a_hi + w1 * b_hi)
        o_ref[:, c_lo] = h_ref[:, c_lo] + (w0 * a_lo + w1 * b_lo)
    if final_norm:
        o_ref[...] = _rms(o_ref[...], gf_ref[...])


def moe_combine(ys, pos, wts, h, final_gain, *, tc=256):
    t, d = h.shape
    lanes = ys.shape[1]
    sub = d // 2 // lanes
    tc = min(tc, t)
    final_norm = final_gain is not None
    gf = (final_gain if final_norm else jnp.ones((d,), F32)).reshape(1, d)
    grid_spec = pltpu.PrefetchScalarGridSpec(
        num_scalar_prefetch=2,
        grid=(t // tc,),
        in_specs=[
            pl.BlockSpec(memory_space=pl.ANY),
            pl.BlockSpec((tc, d), lambda i, p0, p1: (i, 0)),
            pl.BlockSpec((N_EXPERTS, tc), lambda i, p0, p1: (0, i)),
            pl.BlockSpec((1, d), lambda i, p0, p1: (0, 0)),
        ],
        out_specs=pl.BlockSpec((tc, d), lambda i, p0, p1: (i, 0)),
        scratch_shapes=[
            pltpu.VMEM((2, TOP_K, tc * sub, lanes), jnp.uint32),
            pltpu.SemaphoreType.DMA((2,)),
        ],
    )
    return pl.pallas_call(
        functools.partial(_combine_kernel, tc=tc, final_norm=final_norm),
        grid_spec=grid_spec,
        out_shape=jax.ShapeDtypeStruct((t, d), F32),
        compiler_params=_params(("arbitrary",), 48),
        name="moe_combine",
    )(pos[0], pos[1], ys, h, wts, gf)


def moe_block(h, g, w_router, w_gu, w_down, layer, final_gain, *, tm=1024):
    t = h.shape[0]
    tm = min(tm, t)
    xn, idx, wts = router(h, g, w_router)
    pos, tile_expert, tile_rows, n_active = _routing_tables(idx[:TOP_K], tm)
    xs = moe_dispatch(xn, pos, tile_expert.shape[0] * tm)
    ys = moe_experts(xs, tile_expert, tile_rows, n_active, w_gu, w_down, layer, tm=tm)
    return moe_combine(ys, pos, wts, h, final_gain)


def kernel(x, norm_mix, norm_ffn, norm_final, ev_w_in, ev_pool_w, ev_pool_scale, ev_w_out,
           od_w_in, od_conv_w, od_w_out, ffn_w_gu, ffn_w_down, moe_router, moe_w_gu, moe_w_down):
    batch, seq, d = x.shape
    depth = norm_mix.shape[0]
    h = x.reshape(batch * seq, d)
    for layer in range(depth):
        i = layer // 2
        if layer % 2 == 0:
            proj = norm_matmul(h, norm_mix[layer], ev_w_in, i)
            y = even_core(proj, ev_pool_w[i].astype(BF16), ev_pool_scale[i], batch, seq)
            h = matmul_residual(y, ev_w_out[i].astype(BF16), h)
            h = ffn(h, norm_ffn[layer], ffn_w_gu[i].astype(BF16), ffn_w_down[i].astype(BF16))
        else:
            proj = norm_matmul(h, norm_mix[layer], od_w_in, i)
            h = conv_out(proj, od_conv_w[i], od_w_out[i].astype(BF16), h, batch, seq)
            final_gain = norm_final if layer == depth - 1 else None
            h = moe_block(h, norm_ffn[layer], moe_router[i], moe_w_gu, moe_w_down, i, final_gain)
    return h.reshape(batch, seq, d)
```

```python
import functools
import math

import jax
import jax.numpy as jnp
from jax import lax
from jax.experimental import pallas as pl
from jax.experimental.pallas import tpu as pltpu

EPS = 1e-6
CHUNK = 64
POOL_WINDOWS = (2, 4, 8, 16)
POOL_GROUP_DIM = 256
RET_HEADS = 4
RET_HEAD_DIM = 256
ROPE_BASE = 10000.0
N_EXPERTS = 8
TOP_K = 2

RET_BLOCK = 256
POOL_HALO = 128
CONV_HALO = 16
ROUTER_LANES = 128
GATHER_UNROLL = 8
LANES = 128
MOE_SUB_ROWS = 256

BF16 = jnp.bfloat16
F32 = jnp.float32
MIB = 1024 * 1024


def _params(semantics, vmem_mib):
    return pltpu.CompilerParams(dimension_semantics=semantics, vmem_limit_bytes=vmem_mib * MIB)


def _rms(x, g):
    ms = jnp.mean(x * x, axis=-1, keepdims=True)
    return x * lax.rsqrt(ms + EPS) * g


def _silu(x):
    return x / (1.0 + jnp.exp(-x))


def _norm_matmul_kernel(x_ref, g_ref, w_ref, o_ref, xn_ref):
    @pl.when(pl.program_id(1) == 0)
    def _():
        xn_ref[...] = _rms(x_ref[...], g_ref[...]).astype(xn_ref.dtype)

    w = w_ref[...].astype(BF16)
    o_ref[...] = jnp.dot(xn_ref[...], w, preferred_element_type=F32).astype(o_ref.dtype)


def norm_matmul(x, g, w, layer, *, tm=1024, tn=1024):
    t, d = x.shape
    n = w.shape[2]
    tm, tn = min(tm, t), min(tn, n)
    return pl.pallas_call(
        _norm_matmul_kernel,
        grid=(t // tm, n // tn),
        in_specs=[
            pl.BlockSpec((tm, d), lambda i, j: (i, 0)),
            pl.BlockSpec((1, d), lambda i, j: (0, 0)),
            pl.BlockSpec((None, d, tn), lambda i, j: (layer, 0, j)),
        ],
        out_specs=pl.BlockSpec((tm, tn), lambda i, j: (i, j)),
        out_shape=jax.ShapeDtypeStruct((t, n), BF16),
        scratch_shapes=[pltpu.VMEM((tm, d), BF16)],
        compiler_params=_params(("arbitrary", "arbitrary"), 48),
        name="norm_matmul",
    )(x, g.reshape(1, d), w)


def _even_core_kernel(u_ref, uh_ref, q_ref, k_ref, v_ref, g_ref, cos_ref, sin_ref, dm_ref,
                      wp_ref, ps_ref, y_ref, st_ref, *, ts, log_g):
    s = pl.program_id(1)
    gd = POOL_GROUP_DIM
    pw = len(POOL_WINDOWS) * gd

    @pl.when(s == 0)
    def _():
        st_ref[...] = jnp.zeros_like(st_ref)

    row = lax.broadcasted_iota(jnp.int32, (ts, ts), 0)
    col = lax.broadcasted_iota(jnp.int32, (ts, ts), 1)
    dist = row - col
    hrow = lax.broadcasted_iota(jnp.int32, (ts, POOL_HALO), 0)
    hcol = lax.broadcasted_iota(jnp.int32, (ts, POOL_HALO), 1)
    hdist = hrow - hcol + POOL_HALO
    t_seq = s * ts + lax.broadcasted_iota(jnp.int32, (ts, 1), 0)
    uh = jnp.where(s > 0, uh_ref[...], jnp.zeros_like(uh_ref))
    for g, w in enumerate(POOL_WINDOWS):
        cs = slice(g * gd, (g + 1) * gd)
        ug = u_ref[:, cs]
        band = jnp.logical_and(dist >= 0, dist < w).astype(BF16)
        hband = (hdist < w).astype(BF16)
        wsum = (jnp.dot(band, ug, preferred_element_type=F32)
                + jnp.dot(hband, uh[:, cs], preferred_element_type=F32))
        count = jnp.minimum(t_seq + 1, w).astype(F32)
        p = (wsum / count - ug.astype(F32)).astype(BF16)
        yg = jnp.dot(p, wp_ref[g], preferred_element_type=F32) * ps_ref[:, cs]
        y_ref[:, cs] = yg.astype(y_ref.dtype)

    lb = RET_BLOCK
    hd2 = RET_HEAD_DIM // 2
    n_idx = lax.broadcasted_iota(jnp.int32, (lb, 1), 0).astype(F32)
    for hd in range(RET_HEADS):
        lg = log_g[hd]
        q_decay = jnp.exp(lg * (n_idx + 1.0))
        k_decay = jnp.exp(lg * (lb - 1.0 - n_idx))
        block_decay = math.exp(lg * lb)
        c0 = hd * RET_HEAD_DIM
        for r in range(ts // lb):
            rows = slice(r * lb, (r + 1) * lb)
            cs_, sn_ = cos_ref[rows, :], sin_ref[rows, :]

            def rope(ref):
                x1 = ref[rows, c0:c0 + hd2].astype(F32)
                x2 = ref[rows, c0 + hd2:c0 + 2 * hd2].astype(F32)
                return jnp.concatenate([x1 * cs_ - x2 * sn_, x2 * cs_ + x1 * sn_], axis=-1)

            q = rope(q_ref) * (RET_HEAD_DIM ** -0.5)
            k = rope(k_ref)
            v = v_ref[rows, c0:c0 + RET_HEAD_DIM]
            scores = lax.dot_general(q.astype(BF16), k.astype(BF16), (((1,), (1,)), ((), ())),
                                     preferred_element_type=F32) * dm_ref[hd]
            state = st_ref[hd]
            o = (jnp.dot(scores.astype(BF16), v, preferred_element_type=F32)
                 + jnp.dot((q * q_decay).astype(BF16), state.astype(BF16), preferred_element_type=F32))
            st_ref[hd] = state * block_decay + lax.dot_general(
                (k * k_decay).astype(BF16), v, (((0,), (0,)), ((), ())), preferred_element_type=F32)
            mu = jnp.mean(o, axis=-1, keepdims=True)
            oc = o - mu
            var = jnp.mean(oc * oc, axis=-1, keepdims=True)
            gate = g_ref[rows, c0:c0 + RET_HEAD_DIM].astype(F32)
            y_ref[rows, pw + c0:pw + c0 + RET_HEAD_DIM] = (
                oc * lax.rsqrt(var + 1e-5) * _silu(gate)).astype(y_ref.dtype)


def even_core(proj, w_pool, pool_scale, batch, seq, *, ts=512):
    t = proj.shape[0]
    pw = len(POOL_WINDOWS) * POOL_GROUP_DIM
    rw = RET_HEADS * RET_HEAD_DIM
    assert pw == rw and proj.shape[1] == pw + 4 * rw
    ts = min(ts, seq)
    assert ts % RET_BLOCK == 0 and RET_BLOCK % CHUNK == 0 and ts % POOL_HALO == 0
    nsb = seq // ts
    half = RET_HEAD_DIM // 2

    pos = jnp.arange(seq, dtype=F32)
    inv = ROPE_BASE ** (-jnp.arange(half, dtype=F32) / half)
    ang = pos[:, None] * inv[None, :]
    cos, sin = jnp.cos(ang), jnp.sin(ang)
    log_g = tuple(math.log(1.0 - 2.0 ** (-5.0 - h)) for h in range(RET_HEADS))
    idx = jnp.arange(RET_BLOCK)
    visible = (idx[None, :] // CHUNK) <= (idx[:, None] // CHUNK)
    gap = jnp.abs(idx[:, None] - idx[None, :]).astype(F32)
    dmask = jnp.stack([jnp.where(visible, jnp.exp(lg * gap), 0.0) for lg in log_g]).astype(F32)

    def col(c):
        return pl.BlockSpec((ts, pw), lambda b, s: (b * nsb + s, c))

    halo = pl.BlockSpec(
        (POOL_HALO, pw), lambda b, s: (jnp.maximum((b * nsb + s) * (ts // POOL_HALO) - 1, 0), 0))
    return pl.pallas_call(
        functools.partial(_even_core_kernel, ts=ts, log_g=log_g),
        grid=(batch, nsb),
        in_specs=[
            col(0), halo, col(1), col(2), col(3), col(4),
            pl.BlockSpec((ts, half), lambda b, s: (s, 0)),
            pl.BlockSpec((ts, half), lambda b, s: (s, 0)),
            pl.BlockSpec((RET_HEADS, RET_BLOCK, RET_BLOCK), lambda b, s: (0, 0, 0)),
            pl.BlockSpec(w_pool.shape, lambda b, s: (0, 0, 0)),
            pl.BlockSpec((1, pw), lambda b, s: (0, 0)),
        ],
        out_specs=pl.BlockSpec((ts, pw + rw), lambda b, s: (b * nsb + s, 0)),
        out_shape=jax.ShapeDtypeStruct((t, pw + rw), BF16),
        scratch_shapes=[pltpu.VMEM((RET_HEADS, RET_HEAD_DIM, RET_HEAD_DIM), F32)],
        compiler_params=_params(("arbitrary", "arbitrary"), 48),
        name="even_core",
    )(proj, proj, proj, proj, proj, proj, cos, sin, dmask, w_pool, pool_scale.reshape(1, pw))


def _matmul_residual_kernel(y_ref, w_ref, h_ref, o_ref):
    o_ref[...] = h_ref[...] + jnp.dot(y_ref[...], w_ref[...], preferred_element_type=F32)


def matmul_residual(y, w, layer, h, *, tm=512):
    t, k = y.shape
    d = w.shape[2]
    tm = min(tm, t)
    return pl.pallas_call(
        _matmul_residual_kernel,
        grid=(t // tm,),
        in_specs=[
            pl.BlockSpec((tm, k), lambda i: (i, 0)),
            pl.BlockSpec((None, k, d), lambda i: (layer, 0, 0)),
            pl.BlockSpec((tm, d), lambda i: (i, 0)),
        ],
        out_specs=pl.BlockSpec((tm, d), lambda i: (i, 0)),
        out_shape=jax.ShapeDtypeStruct((t, d), F32),
        compiler_params=_params(("arbitrary",), 48),
        name="matmul_residual",
    )(y, w, h)


def _ffn_kernel(x_ref, g_ref, wg_ref, wu_ref, wd_ref, o_ref, xn_ref):
    @pl.when(pl.program_id(1) == 0)
    def _():
        x = x_ref[...]
        xn_ref[...] = _rms(x, g_ref[...]).astype(xn_ref.dtype)
        o_ref[...] = x

    xn = xn_ref[...]
    gate = jnp.dot(xn, wg_ref[...], preferred_element_type=F32)
    up = jnp.dot(xn, wu_ref[...], preferred_element_type=F32)
    act = (_silu(gate) * up).astype(BF16)
    o_ref[...] += jnp.dot(act, wd_ref[...], preferred_element_type=F32)


def ffn(h, g, w_gu, w_down, layer, *, tm=512, tf=512):
    t, d = h.shape
    f = w_down.shape[1]
    tm, tf = min(tm, t), min(tf, f)
    nj = f // tf
    return pl.pallas_call(
        _ffn_kernel,
        grid=(t // tm, nj),
        in_specs=[
            pl.BlockSpec((tm, d), lambda i, j: (i, 0)),
            pl.BlockSpec((1, d), lambda i, j: (0, 0)),
            pl.BlockSpec((None, d, tf), lambda i, j: (layer, 0, j)),
            pl.BlockSpec((None, d, tf), lambda i, j: (layer, 0, j + nj)),
            pl.BlockSpec((None, tf, d), lambda i, j: (layer, j, 0)),
        ],
        out_specs=pl.BlockSpec((tm, d), lambda i, j: (i, 0)),
        out_shape=jax.ShapeDtypeStruct((t, d), F32),
        scratch_shapes=[pltpu.VMEM((tm, d), BF16)],
        compiler_params=_params(("arbitrary", "arbitrary"), 48),
        name="ffn",
    )(h, g.reshape(1, d), w_gu, w_gu, w_down)


def _conv_out_kernel(bg_ref, cg_ref, hx_ref, cgh_ref, hxh_ref, cw_ref, w_ref, h_ref, o_ref, *, ts):
    s = pl.program_id(1)
    z = cg_ref[...].astype(F32) * hx_ref[...].astype(F32)
    zh = cgh_ref[...].astype(F32) * hxh_ref[...].astype(F32)
    zh = jnp.where(s > 0, zh, 0.0)
    prev1 = zh[CONV_HALO - 1:CONV_HALO, :]
    prev2 = zh[CONV_HALO - 2:CONV_HALO - 1, :]
    row = lax.broadcasted_iota(jnp.int32, (ts, 1), 0)
    z1 = jnp.where(row == 0, prev1, pltpu.roll(z, 1, 0))
    z2 = jnp.where(row == 0, prev2, jnp.where(row == 1, prev1, pltpu.roll(z, 2, 0)))
    cw = cw_ref[...]
    conv = z2 * cw[0:1, :] + z1 * cw[1:2, :] + z * cw[2:3, :]
    act = (bg_ref[...].astype(F32) * conv).astype(BF16)
    o_ref[...] = h_ref[...] + jnp.dot(act, w_ref[...], preferred_element_type=F32)


def conv_out(proj, conv_w, w_out, layer, h, batch, seq, *, ts=512):
    t, d = h.shape
    assert proj.shape[1] == 3 * d and conv_w.shape[0] == 3
    ts = min(ts, seq)
    nsb = seq // ts

    def col(c):
        return pl.BlockSpec((ts, d), lambda b, s: (b * nsb + s, c))

    def halo(c):
        return pl.BlockSpec(
            (CONV_HALO, d), lambda b, s: (jnp.maximum((b * nsb + s) * (ts // CONV_HALO) - 1, 0), c))

    return pl.pallas_call(
        functools.partial(_conv_out_kernel, ts=ts),
        grid=(batch, nsb),
        in_specs=[
            col(0), col(1), col(2), halo(1), halo(2),
            pl.BlockSpec(conv_w.shape, lambda b, s: (0, 0)),
            pl.BlockSpec((None, d, d), lambda b, s: (layer, 0, 0)),
            pl.BlockSpec((ts, d), lambda b, s: (b * nsb + s, 0)),
        ],
        out_specs=pl.BlockSpec((ts, d), lambda b, s: (b * nsb + s, 0)),
        out_shape=jax.ShapeDtypeStruct((t, d), F32),
        compiler_params=_params(("arbitrary", "arbitrary"), 56),
        name="conv_out",
    )(proj, proj, proj, proj, proj, conv_w, w_out, h)


def _pack_bf16_pairs(x):
    n = x.shape[1] // 2
    hi = lax.bitcast_convert_type(x[:, :n].astype(BF16).astype(F32), jnp.uint32)
    lo = lax.bitcast_convert_type(x[:, n:].astype(BF16).astype(F32), jnp.uint32)
    return hi | (lo >> 16)


def _unpack_pairs_f32(w):
    hi = lax.bitcast_convert_type(w & jnp.uint32(0xFFFF0000), F32)
    lo = lax.bitcast_convert_type(w << 16, F32)
    return hi, lo


def _store_rows_as_tiles(ref, packed):
    rows = packed.shape[0]
    sub = packed.shape[1] // LANES
    for sl in range(sub):
        ref[pl.ds(sl, rows, stride=sub), :] = packed[:, sl * LANES:(sl + 1) * LANES]


def _load_tile_rows(ref, sl, rows, sub):
    return ref[pl.ds(sl, rows, stride=sub), :]


def _router_kernel(x_ref, g_ref, r_ref, xn_ref, idx_ref, wt_ref):
    xn = _rms(x_ref[...], g_ref[...])
    _store_rows_as_tiles(xn_ref, _pack_bf16_pairs(xn))
    xh = xn.astype(BF16)
    xl = (xn - xh.astype(F32)).astype(BF16)
    ph = jnp.dot(xh, r_ref[...], preferred_element_type=F32)
    pl_ = jnp.dot(xl, r_ref[...], preferred_element_type=F32)
    logits = (ph[:, :ROUTER_LANES] + ph[:, ROUTER_LANES:]) + (pl_[:, :ROUTER_LANES] + pl_[:, ROUTER_LANES:])
    lt = logits.T[:N_EXPERTS, :]
    e_id = lax.broadcasted_iota(jnp.int32, lt.shape, 0)
    m1 = jnp.max(lt, axis=0, keepdims=True)
    i1 = jnp.min(jnp.where(lt == m1, e_id, N_EXPERTS), axis=0, keepdims=True)
    rest = jnp.where(e_id == i1, -jnp.inf, lt)
    m2 = jnp.max(rest, axis=0, keepdims=True)
    i2 = jnp.min(jnp.where(rest == m2, e_id, N_EXPERTS), axis=0, keepdims=True)
    ex = jnp.exp(m2 - m1)
    w1 = 1.0 / (1.0 + ex)
    w2 = ex / (1.0 + ex)
    idx_ref[...] = jnp.where(e_id == 0, i1, jnp.where(e_id == 1, i2, 0))
    wt_ref[...] = jnp.where(e_id == 0, w1, jnp.where(e_id == 1, w2, 0.0))


def router(h, g, w_router, *, tm=512):
    t, d = h.shape
    e = w_router.shape[1]
    assert e == N_EXPERTS
    tm = min(tm, t)
    sub = d // 2 // LANES
    r_pad = jnp.zeros((d, ROUTER_LANES), F32).at[:, :e].set(w_router)
    r_hi = r_pad.astype(BF16)
    r_lo = (r_pad - r_hi.astype(F32)).astype(BF16)
    r_split = jnp.concatenate([r_hi, r_lo], axis=1)
    return pl.pallas_call(
        _router_kernel,
        grid=(t // tm,),
        in_specs=[
            pl.BlockSpec((tm, d), lambda i: (i, 0)),
            pl.BlockSpec((1, d), lambda i: (0, 0)),
            pl.BlockSpec((d, 2 * ROUTER_LANES), lambda i: (0, 0)),
        ],
        out_specs=[
            pl.BlockSpec((tm * sub, LANES), lambda i: (i, 0)),
            pl.BlockSpec((e, tm), lambda i: (0, i)),
            pl.BlockSpec((e, tm), lambda i: (0, i)),
        ],
        out_shape=[
            jax.ShapeDtypeStruct((t * sub, LANES), jnp.uint32),
            jax.ShapeDtypeStruct((e, t), jnp.int32),
            jax.ShapeDtypeStruct((e, t), F32),
        ],
        compiler_params=_params(("arbitrary",), 48),
        name="router",
    )(h, g.reshape(1, d), r_split)


def _routing_tables(idx, tm):
    k, t = idx.shape
    n_tiles = (k * t) // tm + N_EXPERTS
    flat = idx.reshape(-1)
    onehot = (flat[:, None] == jnp.arange(N_EXPERTS)[None, :]).astype(jnp.int32)
    csum = jnp.cumsum(onehot, axis=0)
    rank = jnp.sum(onehot * (csum - 1), axis=1)
    counts = csum[-1]
    padded = ((counts + tm - 1) // tm) * tm
    g_end = jnp.cumsum(padded)
    g_start = g_end - padded
    pos = jnp.sum(onehot * g_start[None, :], axis=1) + rank
    n_active = (g_end[-1] // tm).astype(jnp.int32)
    tile_start = jnp.arange(n_tiles, dtype=jnp.int32) * tm
    tile_expert = jnp.sum((tile_start[:, None] >= g_end[None, :]).astype(jnp.int32), axis=1)
    tile_expert = jnp.minimum(tile_expert, N_EXPERTS - 1)
    tile_rows = jnp.clip((g_start + counts)[tile_expert] - tile_start, 0, tm)
    is_active = jnp.arange(n_tiles) < n_active
    tile_rows = jnp.where(is_active, tile_rows, 0).astype(jnp.int32)
    last = tile_expert[n_active - 1]
    tile_expert = jnp.where(is_active, tile_expert, last).astype(jnp.int32)
    return pos.reshape(k, t).astype(jnp.int32), tile_expert, tile_rows, n_active.reshape(1)


def _dispatch_kernel(p0_ref, p1_ref, x_ref, init_hbm, o_hbm, sem, *, tt, sub):
    del init_hbm
    base = pl.program_id(0) * tt

    def body(r, carry):
        src = x_ref.at[pl.ds(pl.multiple_of(r * sub, sub), sub)]
        for p_ref in (p0_ref, p1_ref):
            dst = pl.ds(pl.multiple_of(p_ref[base + r] * sub, sub), sub)
            pltpu.make_async_copy(src, o_hbm.at[dst], sem).start()
        return carry

    lax.fori_loop(0, tt, body, 0, unroll=GATHER_UNROLL)
    for _ in range(TOP_K):
        pltpu.make_async_copy(x_ref, o_hbm.at[pl.ds(0, tt * sub)], sem).wait()


def moe_dispatch(xn, pos, n_rows, init, *, tt=1024):
    lanes = xn.shape[1]
    t = pos.shape[1]
    sub = xn.shape[0] // t
    tt = min(tt, t)
    grid_spec = pltpu.PrefetchScalarGridSpec(
        num_scalar_prefetch=2,
        grid=(t // tt,),
        in_specs=[
            pl.BlockSpec((tt * sub, lanes), lambda i, p0, p1: (i, 0)),
            pl.BlockSpec(memory_space=pl.ANY),
        ],
        out_specs=pl.BlockSpec(memory_space=pl.ANY),
        scratch_shapes=[pltpu.SemaphoreType.DMA(())],
    )
    return pl.pallas_call(
        functools.partial(_dispatch_kernel, tt=tt, sub=sub),
        grid_spec=grid_spec,
        out_shape=jax.ShapeDtypeStruct((n_rows * sub, lanes), jnp.uint32),
        input_output_aliases={3: 0},
        compiler_params=_params(("arbitrary",), 32),
        name="moe_dispatch",
    )(pos[0], pos[1], xn, jnp.zeros((n_rows * sub, lanes), jnp.uint32) if init is None else init)


def _moe_kernel(te_ref, tr_ref, na_ref, x_ref, wg_ref, wu_ref, wd_ref, o_ref, xb, acc, *, tm, nj):
    j = pl.program_id(1)
    n_rows = tr_ref[pl.program_id(0)]
    half = xb.shape[1] // 2
    sub = half // LANES

    @pl.when(j == 0)
    def _():
        @pl.when(n_rows > 0)
        def _():
            for sl in range(sub):
                hi, lo = _unpack_pairs_f32(_load_tile_rows(x_ref, sl, tm, sub))
                xb[:, sl * LANES:(sl + 1) * LANES] = hi.astype(BF16)
                xb[:, half + sl * LANES:half + (sl + 1) * LANES] = lo.astype(BF16)

        acc[...] = jnp.zeros_like(acc)

    def swiglu_rows(rows, wg, wu, wd):
        x = xb[rows, :]
        gate = jnp.dot(x, wg, preferred_element_type=F32)
        up = jnp.dot(x, wu, preferred_element_type=F32)
        act = (_silu(gate) * up).astype(BF16)
        acc[rows, :] += jnp.dot(act, wd, preferred_element_type=F32)

    @pl.when(n_rows == tm)
    def _():
        swiglu_rows(slice(None), wg_ref[...].astype(BF16), wu_ref[...].astype(BF16), wd_ref[...].astype(BF16))

    @pl.when(jnp.logical_and(n_rows > 0, n_rows < tm))
    def _():
        wg, wu, wd = wg_ref[...].astype(BF16), wu_ref[...].astype(BF16), wd_ref[...].astype(BF16)
        for sb in range(tm // MOE_SUB_ROWS):
            @pl.when(sb * MOE_SUB_ROWS < n_rows)
            def _():
                swiglu_rows(slice(sb * MOE_SUB_ROWS, (sb + 1) * MOE_SUB_ROWS), wg, wu, wd)

    @pl.when(j == nj - 1)
    def _():
        _store_rows_as_tiles(o_ref, _pack_bf16_pairs(acc[...]))


def moe_experts(xs, tile_expert, tile_rows, n_active, w_gu, w_down, layer, *, tm, tf=256):
    lanes = xs.shape[1]
    d = w_gu.shape[2]
    sub = d // 2 // lanes
    f = w_down.shape[2]
    tf = min(tf, f)
    nj = f // tf
    n_tiles = tile_expert.shape[0]
    assert tm % MOE_SUB_ROWS == 0 and xs.shape[0] == n_tiles * tm * sub

    def jj(i, j, na):
        return jnp.where(i < na[0], j, nj - 1)

    def ii(i, na):
        return jnp.minimum(i, na[0] - 1)

    grid_spec = pltpu.PrefetchScalarGridSpec(
        num_scalar_prefetch=3,
        grid=(n_tiles, nj),
        in_specs=[
            pl.BlockSpec((tm * sub, lanes), lambda i, j, te, tr, na: (ii(i, na), 0)),
            pl.BlockSpec((None, None, d, tf), lambda i, j, te, tr, na: (layer, te[i], 0, jj(i, j, na))),
            pl.BlockSpec((None, None, d, tf), lambda i, j, te, tr, na: (layer, te[i], 0, jj(i, j, na) + nj)),
            pl.BlockSpec((None, None, tf, d), lambda i, j, te, tr, na: (layer, te[i], jj(i, j, na), 0)),
        ],
        out_specs=pl.BlockSpec((tm * sub, lanes), lambda i, j, te, tr, na: (i, 0)),
        scratch_shapes=[
            pltpu.VMEM((tm, d), BF16),
            pltpu.VMEM((tm, d), F32),
        ],
    )
    return pl.pallas_call(
        functools.partial(_moe_kernel, tm=tm, nj=nj),
        grid_spec=grid_spec,
        out_shape=jax.ShapeDtypeStruct((n_tiles * tm * sub, lanes), jnp.uint32),
        compiler_params=_params(("arbitrary", "arbitrary"), 56),
        name="moe_experts",
    )(tile_expert, tile_rows, n_active, xs, w_gu, w_gu, w_down)


def _combine_kernel(p0_ref, p1_ref, ys_hbm, h_ref, wt_ref, gf_ref, o_ref, buf, sem, *, tc, final_norm):
    i = pl.program_id(0)
    n = pl.num_programs(0)
    half = o_ref.shape[1] // 2
    sub = half // LANES

    def start_gather(tile, slot):
        base = tile * tc

        def body(r, carry):
            dst = pl.ds(pl.multiple_of(r * sub, sub), sub)
            for kk, p_ref in enumerate((p0_ref, p1_ref)):
                src = pl.ds(pl.multiple_of(p_ref[base + r] * sub, sub), sub)
                pltpu.make_async_copy(ys_hbm.at[src], buf.at[slot, kk, dst], sem.at[slot]).start()
            return carry

        lax.fori_loop(0, tc, body, 0, unroll=GATHER_UNROLL)

    @pl.when(i == 0)
    def _():
        start_gather(0, 0)

    slot = i % 2
    for kk in range(TOP_K):
        pltpu.make_async_copy(ys_hbm.at[pl.ds(0, tc * sub)], buf.at[slot, kk], sem.at[slot]).wait()

    @pl.when(i + 1 < n)
    def _():
        start_gather(i + 1, (i + 1) % 2)

    wt = wt_ref[...].T
    w0, w1 = wt[:, 0:1], wt[:, 1:2]
    for sl in range(sub):
        a_hi, a_lo = _unpack_pairs_f32(_load_tile_rows(buf.at[slot, 0], sl, tc, sub))
        b_hi, b_lo = _unpack_pairs_f32(_load_tile_rows(buf.at[slot, 1], sl, tc, sub))
        c_hi = slice(sl * LANES, (sl + 1) * LANES)
        c_lo = slice(half + sl * LANES, half + (sl + 1) * LANES)
        o_ref[:, c_hi] = h_ref[:, c_hi] + (w0 * a_hi + w1 * b_hi)
        o_ref[:, c_lo] = h_ref[:, c_lo] + (w0 * a_lo + w1 * b_lo)
    if final_norm:
        o_ref[...] = _rms(o_ref[...], gf_ref[...])


def moe_combine(ys, pos, wts, h, final_gain, *, tc=256):
    t, d = h.shape
    lanes = ys.shape[1]
    sub = d // 2 // lanes
    tc = min(tc, t)
    final_norm = final_gain is not None
    gf = (final_gain if final_norm else jnp.ones((d,), F32)).reshape(1, d)
    grid_spec = pltpu.PrefetchScalarGridSpec(
        num_scalar_prefetch=2,
        grid=(t // tc,),
        in_specs=[
            pl.BlockSpec(memory_space=pl.ANY),
            pl.BlockSpec((tc, d), lambda i, p0, p1: (i, 0)),
            pl.BlockSpec((N_EXPERTS, tc), lambda i, p0, p1: (0, i)),
            pl.BlockSpec((1, d), lambda i, p0, p1: (0, 0)),
        ],
        out_specs=pl.BlockSpec((tc, d), lambda i, p0, p1: (i, 0)),
        scratch_shapes=[
            pltpu.VMEM((2, TOP_K, tc * sub, lanes), jnp.uint32),
            pltpu.SemaphoreType.DMA((2,)),
        ],
    )
    return pl.pallas_call(
        functools.partial(_combine_kernel, tc=tc, final_norm=final_norm),
        grid_spec=grid_spec,
        out_shape=jax.ShapeDtypeStruct((t, d), F32),
        compiler_params=_params(("arbitrary",), 48),
        name="moe_combine",
    )(pos[0], pos[1], ys, h, wts, gf)


def moe_block(h, g, w_router, w_gu, w_down, layer, final_gain, xs_prev, *, tm=1024):
    t = h.shape[0]
    tm = min(tm, t)
    xn, idx, wts = router(h, g, w_router)
    pos, tile_expert, tile_rows, n_active = _routing_tables(idx[:TOP_K], tm)
    xs = moe_dispatch(xn, pos, tile_expert.shape[0] * tm, xs_prev)
    ys = moe_experts(xs, tile_expert, tile_rows, n_active, w_gu, w_down, layer, tm=tm)
    return moe_combine(ys, pos, wts, h, final_gain), xs


def kernel(x, norm_mix, norm_ffn, norm_final, ev_w_in, ev_pool_w, ev_pool_scale, ev_w_out,
           od_w_in, od_conv_w, od_w_out, ffn_w_gu, ffn_w_down, moe_router, moe_w_gu, moe_w_down):
    batch, seq, d = x.shape
    depth = norm_mix.shape[0]
    h = x.reshape(batch * seq, d)
    ev_w_out16, od_w_out16 = ev_w_out.astype(BF16), od_w_out.astype(BF16)
    ffn_w_gu16, ffn_w_down16 = ffn_w_gu.astype(BF16), ffn_w_down.astype(BF16)
    xs = None
    for layer in range(depth):
        i = layer // 2
        if layer % 2 == 0:
            proj = norm_matmul(h, norm_mix[layer], ev_w_in, i)
            y = even_core(proj, ev_pool_w[i].astype(BF16), ev_pool_scale[i], batch, seq)
            h = matmul_residual(y, ev_w_out16, i, h)
            h = ffn(h, norm_ffn[layer], ffn_w_gu16, ffn_w_down16, i)
        else:
            proj = norm_matmul(h, norm_mix[layer], od_w_in, i)
            h = conv_out(proj, od_conv_w[i], od_w_out16, i, h, batch, seq)
            final_gain = norm_final if layer == depth - 1 else None
            h, xs = moe_block(h, norm_ffn[layer], moe_router[i], moe_w_gu, moe_w_down, i, final_gain, xs)
    return h.reshape(batch, seq, d)
```

```python
import functools
import math

import jax
import jax.numpy as jnp
from jax import lax
from jax.experimental import pallas as pl
from jax.experimental.pallas import tpu as pltpu

EPS = 1e-6
CHUNK = 64
POOL_WINDOWS = (2, 4, 8, 16)
POOL_GROUP_DIM = 256
RET_HEADS = 4
RET_HEAD_DIM = 256
ROPE_BASE = 10000.0
N_EXPERTS = 8
TOP_K = 2

RET_BLOCK = 256
POOL_HALO = 128
CONV_HALO = 16
ROUTER_LANES = 128
GATHER_UNROLL = 8
LANES = 128
MOE_SUB_ROWS = 256

BF16 = jnp.bfloat16
F32 = jnp.float32
MIB = 1024 * 1024


def _params(semantics, vmem_mib):
    return pltpu.CompilerParams(dimension_semantics=semantics, vmem_limit_bytes=vmem_mib * MIB)


def _rms(x, g):
    ms = jnp.mean(x * x, axis=-1, keepdims=True)
    return x * lax.rsqrt(ms + EPS) * g


def _silu(x):
    return x / (1.0 + jnp.exp(-x))


def _norm_matmul_kernel(x_ref, g_ref, w_ref, o_ref, xn_ref):
    @pl.when(pl.program_id(1) == 0)
    def _():
        xn_ref[...] = _rms(x_ref[...], g_ref[...]).astype(xn_ref.dtype)

    w = w_ref[...].astype(BF16)
    o_ref[...] = jnp.dot(xn_ref[...], w, preferred_element_type=F32).astype(o_ref.dtype)


def norm_matmul(x, g, w, layer, *, tm=1024, tn=1024):
    t, d = x.shape
    n = w.shape[2]
    tm, tn = min(tm, t), min(tn, n)
    return pl.pallas_call(
        _norm_matmul_kernel,
        grid=(t // tm, n // tn),
        in_specs=[
            pl.BlockSpec((tm, d), lambda i, j: (i, 0)),
            pl.BlockSpec((1, d), lambda i, j: (0, 0)),
            pl.BlockSpec((None, d, tn), lambda i, j: (layer, 0, j)),
        ],
        out_specs=pl.BlockSpec((tm, tn), lambda i, j: (i, j)),
        out_shape=jax.ShapeDtypeStruct((t, n), BF16),
        scratch_shapes=[pltpu.VMEM((tm, d), BF16)],
        compiler_params=_params(("arbitrary", "arbitrary"), 48),
        name="norm_matmul",
    )(x, g.reshape(1, d), w)


def _even_core_kernel(u_ref, uh_ref, q_ref, k_ref, v_ref, g_ref, cos_ref, sin_ref, dm_ref,
                      wp_ref, ps_ref, y_ref, st_ref, *, ts, log_g):
    s = pl.program_id(1)
    gd = POOL_GROUP_DIM
    pw = len(POOL_WINDOWS) * gd

    @pl.when(s == 0)
    def _():
        st_ref[...] = jnp.zeros_like(st_ref)

    row = lax.broadcasted_iota(jnp.int32, (ts, ts), 0)
    col = lax.broadcasted_iota(jnp.int32, (ts, ts), 1)
    dist = row - col
    hrow = lax.broadcasted_iota(jnp.int32, (ts, POOL_HALO), 0)
    hcol = lax.broadcasted_iota(jnp.int32, (ts, POOL_HALO), 1)
    hdist = hrow - hcol + POOL_HALO
    t_seq = s * ts + lax.broadcasted_iota(jnp.int32, (ts, 1), 0)
    uh = jnp.where(s > 0, uh_ref[...], jnp.zeros_like(uh_ref))
    for g, w in enumerate(POOL_WINDOWS):
        cs = slice(g * gd, (g + 1) * gd)
        ug = u_ref[:, cs]
        band = jnp.logical_and(dist >= 0, dist < w).astype(BF16)
        hband = (hdist < w).astype(BF16)
        wsum = (jnp.dot(band, ug, preferred_element_type=F32)
                + jnp.dot(hband, uh[:, cs], preferred_element_type=F32))
        count = jnp.minimum(t_seq + 1, w).astype(F32)
        p = (wsum / count - ug.astype(F32)).astype(BF16)
        yg = jnp.dot(p, wp_ref[g], preferred_element_type=F32) * ps_ref[:, cs]
        y_ref[:, cs] = yg.astype(y_ref.dtype)

    lb = RET_BLOCK
    hd2 = RET_HEAD_DIM // 2
    n_idx = lax.broadcasted_iota(jnp.int32, (lb, 1), 0).astype(F32)
    for hd in range(RET_HEADS):
        lg = log_g[hd]
        q_decay = jnp.exp(lg * (n_idx + 1.0))
        k_decay = jnp.exp(lg * (lb - 1.0 - n_idx))
        block_decay = math.exp(lg * lb)
        c0 = hd * RET_HEAD_DIM
        for r in range(ts // lb):
            rows = slice(r * lb, (r + 1) * lb)
            cs_, sn_ = cos_ref[rows, :], sin_ref[rows, :]

            def rope(ref):
                x1 = ref[rows, c0:c0 + hd2].astype(F32)
                x2 = ref[rows, c0 + hd2:c0 + 2 * hd2].astype(F32)
                return jnp.concatenate([x1 * cs_ - x2 * sn_, x2 * cs_ + x1 * sn_], axis=-1)

            q = rope(q_ref) * (RET_HEAD_DIM ** -0.5)
            k = rope(k_ref)
            v = v_ref[rows, c0:c0 + RET_HEAD_DIM]
            scores = lax.dot_general(q.astype(BF16), k.astype(BF16), (((1,), (1,)), ((), ())),
                                     preferred_element_type=F32) * dm_ref[hd]
            state = st_ref[hd]
            o = (jnp.dot(scores.astype(BF16), v, preferred_element_type=F32)
                 + jnp.dot((q * q_decay).astype(BF16), state.astype(BF16), preferred_element_type=F32))
            st_ref[hd] = state * block_decay + lax.dot_general(
                (k * k_decay).astype(BF16), v, (((0,), (0,)), ((), ())), preferred_element_type=F32)
            mu = jnp.mean(o, axis=-1, keepdims=True)
            oc = o - mu
            var = jnp.mean(oc * oc, axis=-1, keepdims=True)
            gate = g_ref[rows, c0:c0 + RET_HEAD_DIM].astype(F32)
            y_ref[rows, pw + c0:pw + c0 + RET_HEAD_DIM] = (
                oc * lax.rsqrt(var + 1e-5) * _silu(gate)).astype(y_ref.dtype)


def even_core(proj, w_pool, pool_scale, batch, seq, *, ts=512):
    t = proj.shape[0]
    pw = len(POOL_WINDOWS) * POOL_GROUP_DIM
    rw = RET_HEADS * RET_HEAD_DIM
    assert pw == rw and proj.shape[1] == pw + 4 * rw
    ts = min(ts, seq)
    assert ts % RET_BLOCK == 0 and RET_BLOCK % CHUNK == 0 and ts % POOL_HALO == 0
    nsb = seq // ts
    half = RET_HEAD_DIM // 2

    pos = jnp.arange(seq, dtype=F32)
    inv = ROPE_BASE ** (-jnp.arange(half, dtype=F32) / half)
    ang = pos[:, None] * inv[None, :]
    cos, sin = jnp.cos(ang), jnp.sin(ang)
    log_g = tuple(math.log(1.0 - 2.0 ** (-5.0 - h)) for h in range(RET_HEADS))
    idx = jnp.arange(RET_BLOCK)
    visible = (idx[None, :] // CHUNK) <= (idx[:, None] // CHUNK)
    gap = jnp.abs(idx[:, None] - idx[None, :]).astype(F32)
    dmask = jnp.stack([jnp.where(visible, jnp.exp(lg * gap), 0.0) for lg in log_g]).astype(F32)

    def col(c):
        return pl.BlockSpec((ts, pw), lambda b, s: (b * nsb + s, c))

    halo = pl.BlockSpec(
        (POOL_HALO, pw), lambda b, s: (jnp.maximum((b * nsb + s) * (ts // POOL_HALO) - 1, 0), 0))
    return pl.pallas_call(
        functools.partial(_even_core_kernel, ts=ts, log_g=log_g),
        grid=(batch, nsb),
        in_specs=[
            col(0), halo, col(1), col(2), col(3), col(4),
            pl.BlockSpec((ts, half), lambda b, s: (s, 0)),
            pl.BlockSpec((ts, half), lambda b, s: (s, 0)),
            pl.BlockSpec((RET_HEADS, RET_BLOCK, RET_BLOCK), lambda b, s: (0, 0, 0)),
            pl.BlockSpec(w_pool.shape, lambda b, s: (0, 0, 0)),
            pl.BlockSpec((1, pw), lambda b, s: (0, 0)),
        ],
        out_specs=pl.BlockSpec((ts, pw + rw), lambda b, s: (b * nsb + s, 0)),
        out_shape=jax.ShapeDtypeStruct((t, pw + rw), BF16),
        scratch_shapes=[pltpu.VMEM((RET_HEADS, RET_HEAD_DIM, RET_HEAD_DIM), F32)],
        compiler_params=_params(("arbitrary", "arbitrary"), 48),
        name="even_core",
    )(proj, proj, proj, proj, proj, proj, cos, sin, dmask, w_pool, pool_scale.reshape(1, pw))


def _matmul_residual_kernel(y_ref, w_ref, h_ref, o_ref):
    o_ref[...] = h_ref[...] + jnp.dot(y_ref[...], w_ref[...], preferred_element_type=F32)


def matmul_residual(y, w, layer, h, *, tm=512):
    t, k = y.shape
    d = w.shape[2]
    tm = min(tm, t)
    return pl.pallas_call(
        _matmul_residual_kernel,
        grid=(t // tm,),
        in_specs=[
            pl.BlockSpec((tm, k), lambda i: (i, 0)),
            pl.BlockSpec((None, k, d), lambda i: (layer, 0, 0)),
            pl.BlockSpec((tm, d), lambda i: (i, 0)),
        ],
        out_specs=pl.BlockSpec((tm, d), lambda i: (i, 0)),
        out_shape=jax.ShapeDtypeStruct((t, d), F32),
        compiler_params=_params(("arbitrary",), 48),
        name="matmul_residual",
    )(y, w, h)


def _ffn_kernel(x_ref, g_ref, wg_ref, wu_ref, wd_ref, o_ref, xn_ref):
    @pl.when(pl.program_id(1) == 0)
    def _():
        x = x_ref[...]
        xn_ref[...] = _rms(x, g_ref[...]).astype(xn_ref.dtype)
        o_ref[...] = x

    xn = xn_ref[...]
    gate = jnp.dot(xn, wg_ref[...], preferred_element_type=F32)
    up = jnp.dot(xn, wu_ref[...], preferred_element_type=F32)
    act = (_silu(gate) * up).astype(BF16)
    o_ref[...] += jnp.dot(act, wd_ref[...], preferred_element_type=F32)


def ffn(h, g, w_gu, w_down, layer, *, tm=1024, tf=512):
    t, d = h.shape
    f = w_down.shape[1]
    tm, tf = min(tm, t), min(tf, f)
    nj = f // tf
    return pl.pallas_call(
        _ffn_kernel,
        grid=(t // tm, nj),
        in_specs=[
            pl.BlockSpec((tm, d), lambda i, j: (i, 0)),
            pl.BlockSpec((1, d), lambda i, j: (0, 0)),
            pl.BlockSpec((None, d, tf), lambda i, j: (layer, 0, j)),
            pl.BlockSpec((None, d, tf), lambda i, j: (layer, 0, j + nj)),
            pl.BlockSpec((None, tf, d), lambda i, j: (layer, j, 0)),
        ],
        out_specs=pl.BlockSpec((tm, d), lambda i, j: (i, 0)),
        out_shape=jax.ShapeDtypeStruct((t, d), F32),
        scratch_shapes=[pltpu.VMEM((tm, d), BF16)],
        compiler_params=_params(("arbitrary", "arbitrary"), 56),
        name="ffn",
    )(h, g.reshape(1, d), w_gu, w_gu, w_down)


def _conv_out_kernel(bg_ref, cg_ref, hx_ref, cgh_ref, hxh_ref, cw_ref, w_ref, h_ref, o_ref, *, ts):
    s = pl.program_id(1)
    z = cg_ref[...].astype(F32) * hx_ref[...].astype(F32)
    zh = cgh_ref[...].astype(F32) * hxh_ref[...].astype(F32)
    zh = jnp.where(s > 0, zh, 0.0)
    prev1 = zh[CONV_HALO - 1:CONV_HALO, :]
    prev2 = zh[CONV_HALO - 2:CONV_HALO - 1, :]
    row = lax.broadcasted_iota(jnp.int32, (ts, 1), 0)
    z1 = jnp.where(row == 0, prev1, pltpu.roll(z, 1, 0))
    z2 = jnp.where(row == 0, prev2, jnp.where(row == 1, prev1, pltpu.roll(z, 2, 0)))
    cw = cw_ref[...]
    conv = z2 * cw[0:1, :] + z1 * cw[1:2, :] + z * cw[2:3, :]
    act = (bg_ref[...].astype(F32) * conv).astype(BF16)
    o_ref[...] = h_ref[...] + jnp.dot(act, w_ref[...], preferred_element_type=F32)


def conv_out(proj, conv_w, w_out, layer, h, batch, seq, *, ts=512):
    t, d = h.shape
    assert proj.shape[1] == 3 * d and conv_w.shape[0] == 3
    ts = min(ts, seq)
    nsb = seq // ts

    def col(c):
        return pl.BlockSpec((ts, d), lambda b, s: (b * nsb + s, c))

    def halo(c):
        return pl.BlockSpec(
            (CONV_HALO, d), lambda b, s: (jnp.maximum((b * nsb + s) * (ts // CONV_HALO) - 1, 0), c))

    return pl.pallas_call(
        functools.partial(_conv_out_kernel, ts=ts),
        grid=(batch, nsb),
        in_specs=[
            col(0), col(1), col(2), halo(1), halo(2),
            pl.BlockSpec(conv_w.shape, lambda b, s: (0, 0)),
            pl.BlockSpec((None, d, d), lambda b, s: (layer, 0, 0)),
            pl.BlockSpec((ts, d), lambda b, s: (b * nsb + s, 0)),
        ],
        out_specs=pl.BlockSpec((ts, d), lambda b, s: (b * nsb + s, 0)),
        out_shape=jax.ShapeDtypeStruct((t, d), F32),
        compiler_params=_params(("arbitrary", "arbitrary"), 56),
        name="conv_out",
    )(proj, proj, proj, proj, proj, conv_w, w_out, h)


def _pack_bf16_pairs(x):
    n = x.shape[1] // 2
    hi = lax.bitcast_convert_type(x[:, :n].astype(BF16).astype(F32), jnp.uint32)
    lo = lax.bitcast_convert_type(x[:, n:].astype(BF16).astype(F32), jnp.uint32)
    return hi | (lo >> 16)


def _unpack_pairs_f32(w):
    hi = lax.bitcast_convert_type(w & jnp.uint32(0xFFFF0000), F32)
    lo = lax.bitcast_convert_type(w << 16, F32)
    return hi, lo


def _store_rows_as_tiles(ref, packed):
    rows = packed.shape[0]
    sub = packed.shape[1] // LANES
    for sl in range(sub):
        ref[pl.ds(sl, rows, stride=sub), :] = packed[:, sl * LANES:(sl + 1) * LANES]


def _load_tile_rows(ref, sl, rows, sub):
    return ref[pl.ds(sl, rows, stride=sub), :]


def _router_kernel(x_ref, g_ref, r_ref, xn_ref, idx_ref, wt_ref):
    xn = _rms(x_ref[...], g_ref[...])
    _store_rows_as_tiles(xn_ref, _pack_bf16_pairs(xn))
    xh = xn.astype(BF16)
    xl = (xn - xh.astype(F32)).astype(BF16)
    ph = jnp.dot(xh, r_ref[...], preferred_element_type=F32)
    pl_ = jnp.dot(xl, r_ref[...], preferred_element_type=F32)
    logits = (ph[:, :ROUTER_LANES] + ph[:, ROUTER_LANES:]) + (pl_[:, :ROUTER_LANES] + pl_[:, ROUTER_LANES:])
    lt = logits.T[:N_EXPERTS, :]
    e_id = lax.broadcasted_iota(jnp.int32, lt.shape, 0)
    m1 = jnp.max(lt, axis=0, keepdims=True)
    i1 = jnp.min(jnp.where(lt == m1, e_id, N_EXPERTS), axis=0, keepdims=True)
    rest = jnp.where(e_id == i1, -jnp.inf, lt)
    m2 = jnp.max(rest, axis=0, keepdims=True)
    i2 = jnp.min(jnp.where(rest == m2, e_id, N_EXPERTS), axis=0, keepdims=True)
    ex = jnp.exp(m2 - m1)
    w1 = 1.0 / (1.0 + ex)
    w2 = ex / (1.0 + ex)
    idx_ref[...] = jnp.where(e_id == 0, i1, jnp.where(e_id == 1, i2, 0))
    wt_ref[...] = jnp.where(e_id == 0, w1, jnp.where(e_id == 1, w2, 0.0))


def router(h, g, w_router, *, tm=512):
    t, d = h.shape
    e = w_router.shape[1]
    assert e == N_EXPERTS
    tm = min(tm, t)
    sub = d // 2 // LANES
    r_pad = jnp.zeros((d, ROUTER_LANES), F32).at[:, :e].set(w_router)
    r_hi = r_pad.astype(BF16)
    r_lo = (r_pad - r_hi.astype(F32)).astype(BF16)
    r_split = jnp.concatenate([r_hi, r_lo], axis=1)
    return pl.pallas_call(
        _router_kernel,
        grid=(t // tm,),
        in_specs=[
            pl.BlockSpec((tm, d), lambda i: (i, 0)),
            pl.BlockSpec((1, d), lambda i: (0, 0)),
            pl.BlockSpec((d, 2 * ROUTER_LANES), lambda i: (0, 0)),
        ],
        out_specs=[
            pl.BlockSpec((tm * sub, LANES), lambda i: (i, 0)),
            pl.BlockSpec((e, tm), lambda i: (0, i)),
            pl.BlockSpec((e, tm), lambda i: (0, i)),
        ],
        out_shape=[
            jax.ShapeDtypeStruct((t * sub, LANES), jnp.uint32),
            jax.ShapeDtypeStruct((e, t), jnp.int32),
            jax.ShapeDtypeStruct((e, t), F32),
        ],
        compiler_params=_params(("arbitrary",), 48),
        name="router",
    )(h, g.reshape(1, d), r_split)


def _routing_tables(idx, tm):
    k, t = idx.shape
    n_tiles = (k * t) // tm + N_EXPERTS
    flat = idx.reshape(-1)
    onehot = (flat[:, None] == jnp.arange(N_EXPERTS)[None, :]).astype(jnp.int32)
    csum = jnp.cumsum(onehot, axis=0)
    rank = jnp.sum(onehot * (csum - 1), axis=1)
    counts = csum[-1]
    padded = ((counts + tm - 1) // tm) * tm
    g_end = jnp.cumsum(padded)
    g_start = g_end - padded
    pos = jnp.sum(onehot * g_start[None, :], axis=1) + rank
    n_active = (g_end[-1] // tm).astype(jnp.int32)
    tile_start = jnp.arange(n_tiles, dtype=jnp.int32) * tm
    tile_expert = jnp.sum((tile_start[:, None] >= g_end[None, :]).astype(jnp.int32), axis=1)
    tile_expert = jnp.minimum(tile_expert, N_EXPERTS - 1)
    tile_rows = jnp.clip((g_start + counts)[tile_expert] - tile_start, 0, tm)
    is_active = jnp.arange(n_tiles) < n_active
    tile_rows = jnp.where(is_active, tile_rows, 0).astype(jnp.int32)
    last = tile_expert[n_active - 1]
    tile_expert = jnp.where(is_active, tile_expert, last).astype(jnp.int32)
    return pos.reshape(k, t).astype(jnp.int32), tile_expert, tile_rows, n_active.reshape(1)


def _dispatch_kernel(p0_ref, p1_ref, x_ref, init_hbm, o_hbm, sem, *, tt, sub):
    del init_hbm
    base = pl.program_id(0) * tt

    def body(r, carry):
        src = x_ref.at[pl.ds(pl.multiple_of(r * sub, sub), sub)]
        for p_ref in (p0_ref, p1_ref):
            dst = pl.ds(pl.multiple_of(p_ref[base + r] * sub, sub), sub)
            pltpu.make_async_copy(src, o_hbm.at[dst], sem).start()
        return carry

    lax.fori_loop(0, tt, body, 0, unroll=GATHER_UNROLL)
    for _ in range(TOP_K):
        pltpu.make_async_copy(x_ref, o_hbm.at[pl.ds(0, tt * sub)], sem).wait()


def moe_dispatch(xn, pos, n_rows, init, *, tt=1024):
    lanes = xn.shape[1]
    t = pos.shape[1]
    sub = xn.shape[0] // t
    tt = min(tt, t)
    grid_spec = pltpu.PrefetchScalarGridSpec(
        num_scalar_prefetch=2,
        grid=(t // tt,),
        in_specs=[
            pl.BlockSpec((tt * sub, lanes), lambda i, p0, p1: (i, 0)),
            pl.BlockSpec(memory_space=pl.ANY),
        ],
        out_specs=pl.BlockSpec(memory_space=pl.ANY),
        scratch_shapes=[pltpu.SemaphoreType.DMA(())],
    )
    return pl.pallas_call(
        functools.partial(_dispatch_kernel, tt=tt, sub=sub),
        grid_spec=grid_spec,
        out_shape=jax.ShapeDtypeStruct((n_rows * sub, lanes), jnp.uint32),
        input_output_aliases={3: 0},
        compiler_params=_params(("arbitrary",), 32),
        name="moe_dispatch",
    )(pos[0], pos[1], xn, jnp.zeros((n_rows * sub, lanes), jnp.uint32) if init is None else init)


def _moe_kernel(te_ref, tr_ref, na_ref, x_ref, wg_ref, wu_ref, wd_ref, o_ref, xb, acc, *, tm, nj):
    j = pl.program_id(1)
    n_rows = tr_ref[pl.program_id(0)]
    half = xb.shape[1] // 2
    sub = half // LANES
    whole = n_rows > tm - MOE_SUB_ROWS

    @pl.when(j == 0)
    def _():
        @pl.when(n_rows > 0)
        def _():
            for sl in range(sub):
                hi, lo = _unpack_pairs_f32(_load_tile_rows(x_ref, sl, tm, sub))
                xb[:, sl * LANES:(sl + 1) * LANES] = hi.astype(BF16)
                xb[:, half + sl * LANES:half + (sl + 1) * LANES] = lo.astype(BF16)

        @pl.when(jnp.logical_not(whole))
        def _():
            acc[...] = jnp.zeros_like(acc)

    def swiglu_rows(rows, wg, wu, wd, assign=False):
        x = xb[rows, :]
        gate = jnp.dot(x, wg, preferred_element_type=F32)
        up = jnp.dot(x, wu, preferred_element_type=F32)
        act = (_silu(gate) * up).astype(BF16)
        y = jnp.dot(act, wd, preferred_element_type=F32)
        if assign:
            acc[rows, :] = y
        else:
            acc[rows, :] += y

    for first in (True, False):
        @pl.when(jnp.logical_and(whole, (j == 0) if first else (j > 0)))
        def _():
            swiglu_rows(slice(None), wg_ref[...].astype(BF16), wu_ref[...].astype(BF16),
                        wd_ref[...].astype(BF16), assign=first)

    @pl.when(jnp.logical_and(n_rows > 0, jnp.logical_not(whole)))
    def _():
        wg, wu, wd = wg_ref[...].astype(BF16), wu_ref[...].astype(BF16), wd_ref[...].astype(BF16)
        for sb in range(tm // MOE_SUB_ROWS):
            @pl.when(sb * MOE_SUB_ROWS < n_rows)
            def _():
                swiglu_rows(slice(sb * MOE_SUB_ROWS, (sb + 1) * MOE_SUB_ROWS), wg, wu, wd)

    @pl.when(j == nj - 1)
    def _():
        _store_rows_as_tiles(o_ref, _pack_bf16_pairs(acc[...]))


def moe_experts(xs, tile_expert, tile_rows, n_active, w_gu, w_down, layer, *, tm, tf=256):
    lanes = xs.shape[1]
    d = w_gu.shape[2]
    sub = d // 2 // lanes
    f = w_down.shape[2]
    tf = min(tf, f)
    nj = f // tf
    n_tiles = tile_expert.shape[0]
    assert tm % MOE_SUB_ROWS == 0 and xs.shape[0] == n_tiles * tm * sub

    def jj(i, j, na):
        return jnp.where(i < na[0], j, nj - 1)

    def ii(i, na):
        return jnp.minimum(i, na[0] - 1)

    grid_spec = pltpu.PrefetchScalarGridSpec(
        num_scalar_prefetch=3,
        grid=(n_tiles, nj),
        in_specs=[
            pl.BlockSpec((tm * sub, lanes), lambda i, j, te, tr, na: (ii(i, na), 0)),
            pl.BlockSpec((None, None, d, tf), lambda i, j, te, tr, na: (layer, te[i], 0, jj(i, j, na))),
            pl.BlockSpec((None, None, d, tf), lambda i, j, te, tr, na: (layer, te[i], 0, jj(i, j, na) + nj)),
            pl.BlockSpec((None, None, tf, d), lambda i, j, te, tr, na: (layer, te[i], jj(i, j, na), 0)),
        ],
        out_specs=pl.BlockSpec((tm * sub, lanes), lambda i, j, te, tr, na: (i, 0)),
        scratch_shapes=[
            pltpu.VMEM((tm, d), BF16),
            pltpu.VMEM((tm, d), F32),
        ],
    )
    return pl.pallas_call(
        functools.partial(_moe_kernel, tm=tm, nj=nj),
        grid_spec=grid_spec,
        out_shape=jax.ShapeDtypeStruct((n_tiles * tm * sub, lanes), jnp.uint32),
        compiler_params=_params(("arbitrary", "arbitrary"), 56),
        name="moe_experts",
    )(tile_expert, tile_rows, n_active, xs, w_gu, w_gu, w_down)


def _combine_kernel(p0_ref, p1_ref, ys_hbm, h_ref, wt_ref, gf_ref, o_ref, buf, sem, *, tc, final_norm):
    i = pl.program_id(0)
    n = pl.num_programs(0)
    half = o_ref.shape[1] // 2
    sub = half // LANES

    def start_gather(tile, slot):
        base = tile * tc

        def body(r, carry):
            dst = pl.ds(pl.multiple_of(r * sub, sub), sub)
            for kk, p_ref in enumerate((p0_ref, p1_ref)):
                src = pl.ds(pl.multiple_of(p_ref[base + r] * sub, sub), sub)
                pltpu.make_async_copy(ys_hbm.at[src], buf.at[slot, kk, dst], sem.at[slot]).start()
            return carry

        lax.fori_loop(0, tc, body, 0, unroll=GATHER_UNROLL)

    @pl.when(i == 0)
    def _():
        start_gather(0, 0)

    slot = i % 2
    for kk in range(TOP_K):
        pltpu.make_async_copy(ys_hbm.at[pl.ds(0, tc * sub)], buf.at[slot, kk], sem.at[slot]).wait()

    @pl.when(i + 1 < n)
    def _():
        start_gather(i + 1, (i + 1) % 2)

    wt = wt_ref[...].T
    w0, w1 = wt[:, 0:1], wt[:, 1:2]
    for sl in range(sub):
        a_hi, a_lo = _unpack_pairs_f32(_load_tile_rows(buf.at[slot, 0], sl, tc, sub))
        b_hi, b_lo = _unpack_pairs_f32(_load_tile_rows(buf.at[slot, 1], sl, tc, sub))
        c_hi = slice(sl * LANES, (sl + 1) * LANES)
        c_lo = slice(half + sl * LANES, half + (sl + 1) * LANES)
        o_ref[:, c_hi] = h_ref[:, c_hi] + (w0 * a_hi + w1 * b_hi)
        o_ref[:, c_lo] = h_ref[:, c_lo] + (w0 * a_lo + w1 * b_lo)
    if final_norm:
        o_ref[...] = _rms(o_ref[...], gf_ref[...])


def moe_combine(ys, pos, wts, h, final_gain, *, tc=256):
    t, d = h.shape
    lanes = ys.shape[1]
    sub = d // 2 // lanes
    tc = min(tc, t)
    final_norm = final_gain is not None
    gf = (final_gain if final_norm else jnp.ones((d,), F32)).reshape(1, d)
    grid_spec = pltpu.PrefetchScalarGridSpec(
        num_scalar_prefetch=2,
        grid=(t // tc,),
        in_specs=[
            pl.BlockSpec(memory_space=pl.ANY),
            pl.BlockSpec((tc, d), lambda i, p0, p1: (i, 0)),
            pl.BlockSpec((N_EXPERTS, tc), lambda i, p0, p1: (0, i)),
            pl.BlockSpec((1, d), lambda i, p0, p1: (0, 0)),
        ],
        out_specs=pl.BlockSpec((tc, d), lambda i, p0, p1: (i, 0)),
        scratch_shapes=[
            pltpu.VMEM((2, TOP_K, tc * sub, lanes), jnp.uint32),
            pltpu.SemaphoreType.DMA((2,)),
        ],
    )
    return pl.pallas_call(
        functools.partial(_combine_kernel, tc=tc, final_norm=final_norm),
        grid_spec=grid_spec,
        out_shape=jax.ShapeDtypeStruct((t, d), F32),
        compiler_params=_params(("arbitrary",), 48),
        name="moe_combine",
    )(pos[0], pos[1], ys, h, wts, gf)


def moe_block(h, g, w_router, w_gu, w_down, layer, final_gain, xs_prev, *, tm=1024):
    t = h.shape[0]
    tm = min(tm, t)
    xn, idx, wts = router(h, g, w_router)
    pos, tile_expert, tile_rows, n_active = _routing_tables(idx[:TOP_K], tm)
    xs = moe_dispatch(xn, pos, tile_expert.shape[0] * tm, xs_prev)
    ys = moe_experts(xs, tile_expert, tile_rows, n_active, w_gu, w_down, layer, tm=tm)
    return moe_combine(ys, pos, wts, h, final_gain), xs


def kernel(x, norm_mix, norm_ffn, norm_final, ev_w_in, ev_pool_w, ev_pool_scale, ev_w_out,
           od_w_in, od_conv_w, od_w_out, ffn_w_gu, ffn_w_down, moe_router, moe_w_gu, moe_w_down):
    batch, seq, d = x.shape
    depth = norm_mix.shape[0]
    h = x.reshape(batch * seq, d)
    ev_w_out16, od_w_out16 = ev_w_out.astype(BF16), od_w_out.astype(BF16)
    ffn_w_gu16, ffn_w_down16 = ffn_w_gu.astype(BF16), ffn_w_down.astype(BF16)
    xs = None
    for layer in range(depth):
        i = layer // 2
        if layer % 2 == 0:
            proj = norm_matmul(h, norm_mix[layer], ev_w_in, i)
            y = even_core(proj, ev_pool_w[i].astype(BF16), ev_pool_scale[i], batch, seq)
            h = matmul_residual(y, ev_w_out16, i, h)
            h = ffn(h, norm_ffn[layer], ffn_w_gu16, ffn_w_down16, i)
        else:
            proj = norm_matmul(h, norm_mix[layer], od_w_in, i)
            h = conv_out(proj, od_conv_w[i], od_w_out16, i, h, batch, seq)
            final_gain = norm_final if layer == depth - 1 else None
            h, xs = moe_block(h, norm_ffn[layer], moe_router[i], moe_w_gu, moe_w_down, i, final_gain, xs)
    return h.reshape(batch, seq, d)
```

```python
import functools
import math

import jax
import jax.numpy as jnp
from jax import lax
from jax.experimental import pallas as pl
from jax.experimental.pallas import tpu as pltpu

EPS = 1e-6
CHUNK = 64
POOL_WINDOWS = (2, 4, 8, 16)
POOL_GROUP_DIM = 256
RET_HEADS = 4
RET_HEAD_DIM = 256
ROPE_BASE = 10000.0
N_EXPERTS = 8
TOP_K = 2

RET_BLOCK = 256
POOL_HALO = 128
CONV_HALO = 16
ROUTER_LANES = 128
GATHER_UNROLL = 8
LANES = 128
MOE_SUB_ROWS = 256

BF16 = jnp.bfloat16
F32 = jnp.float32
MIB = 1024 * 1024


def _params(semantics, vmem_mib):
    return pltpu.CompilerParams(dimension_semantics=semantics, vmem_limit_bytes=vmem_mib * MIB)


def _rms(x, g):
    ms = jnp.mean(x * x, axis=-1, keepdims=True)
    return x * lax.rsqrt(ms + EPS) * g


def _silu(x):
    return x / (1.0 + jnp.exp(-x))


def _norm_matmul_kernel(x_ref, g_ref, w_ref, o_ref, xn_ref):
    @pl.when(pl.program_id(1) == 0)
    def _():
        xn_ref[...] = _rms(x_ref[...], g_ref[...]).astype(xn_ref.dtype)

    w = w_ref[...].astype(BF16)
    o_ref[...] = jnp.dot(xn_ref[...], w, preferred_element_type=F32).astype(o_ref.dtype)


def norm_matmul(x, g, w, layer, *, tm=1024, tn=1024):
    t, d = x.shape
    n = w.shape[2]
    tm, tn = min(tm, t), min(tn, n)
    return pl.pallas_call(
        _norm_matmul_kernel,
        grid=(t // tm, n // tn),
        in_specs=[
            pl.BlockSpec((tm, d), lambda i, j: (i, 0)),
            pl.BlockSpec((1, d), lambda i, j: (0, 0)),
            pl.BlockSpec((None, d, tn), lambda i, j: (layer, 0, j)),
        ],
        out_specs=pl.BlockSpec((tm, tn), lambda i, j: (i, j)),
        out_shape=jax.ShapeDtypeStruct((t, n), BF16),
        scratch_shapes=[pltpu.VMEM((tm, d), BF16)],
        compiler_params=_params(("arbitrary", "arbitrary"), 48),
        name="norm_matmul",
    )(x, g.reshape(1, d), w)


def _even_core_kernel(u_ref, uh_ref, q_ref, k_ref, v_ref, g_ref, cos_ref, sin_ref, dm_ref,
                      wp_ref, ps_ref, y_ref, st_ref, *, ts, log_g):
    s = pl.program_id(1)
    gd = POOL_GROUP_DIM
    pw = len(POOL_WINDOWS) * gd

    @pl.when(s == 0)
    def _():
        st_ref[...] = jnp.zeros_like(st_ref)

    row = lax.broadcasted_iota(jnp.int32, (ts, ts), 0)
    col = lax.broadcasted_iota(jnp.int32, (ts, ts), 1)
    dist = row - col
    hrow = lax.broadcasted_iota(jnp.int32, (ts, POOL_HALO), 0)
    hcol = lax.broadcasted_iota(jnp.int32, (ts, POOL_HALO), 1)
    hdist = hrow - hcol + POOL_HALO
    t_seq = s * ts + lax.broadcasted_iota(jnp.int32, (ts, 1), 0)
    uh = jnp.where(s > 0, uh_ref[...], jnp.zeros_like(uh_ref))
    for g, w in enumerate(POOL_WINDOWS):
        cs = slice(g * gd, (g + 1) * gd)
        ug = u_ref[:, cs]
        band = jnp.logical_and(dist >= 0, dist < w).astype(BF16)
        hband = (hdist < w).astype(BF16)
        wsum = (jnp.dot(band, ug, preferred_element_type=F32)
                + jnp.dot(hband, uh[:, cs], preferred_element_type=F32))
        count = jnp.minimum(t_seq + 1, w).astype(F32)
        p = (wsum / count - ug.astype(F32)).astype(BF16)
        yg = jnp.dot(p, wp_ref[g], preferred_element_type=F32) * ps_ref[:, cs]
        y_ref[:, cs] = yg.astype(y_ref.dtype)

    lb = RET_BLOCK
    hd2 = RET_HEAD_DIM // 2
    n_idx = lax.broadcasted_iota(jnp.int32, (lb, 1), 0).astype(F32)
    for hd in range(RET_HEADS):
        lg = log_g[hd]
        q_decay = jnp.exp(lg * (n_idx + 1.0))
        k_decay = jnp.exp(lg * (lb - 1.0 - n_idx))
        block_decay = math.exp(lg * lb)
        c0 = hd * RET_HEAD_DIM
        for r in range(ts // lb):
            rows = slice(r * lb, (r + 1) * lb)
            cs_, sn_ = cos_ref[rows, :], sin_ref[rows, :]

            def rope(ref):
                x1 = ref[rows, c0:c0 + hd2].astype(F32)
                x2 = ref[rows, c0 + hd2:c0 + 2 * hd2].astype(F32)
                return jnp.concatenate([x1 * cs_ - x2 * sn_, x2 * cs_ + x1 * sn_], axis=-1)

            q = rope(q_ref) * (RET_HEAD_DIM ** -0.5)
            k = rope(k_ref)
            v = v_ref[rows, c0:c0 + RET_HEAD_DIM]
            scores = lax.dot_general(q.astype(BF16), k.astype(BF16), (((1,), (1,)), ((), ())),
                                     preferred_element_type=F32) * dm_ref[hd]
            state = st_ref[hd]
            o = (jnp.dot(scores.astype(BF16), v, preferred_element_type=F32)
                 + jnp.dot((q * q_decay).astype(BF16), state.astype(BF16), preferred_element_type=F32))
            st_ref[hd] = state * block_decay + lax.dot_general(
                (k * k_decay).astype(BF16), v, (((0,), (0,)), ((), ())), preferred_element_type=F32)
            mu = jnp.mean(o, axis=-1, keepdims=True)
            oc = o - mu
            var = jnp.mean(oc * oc, axis=-1, keepdims=True)
            gate = g_ref[rows, c0:c0 + RET_HEAD_DIM].astype(F32)
            y_ref[rows, pw + c0:pw + c0 + RET_HEAD_DIM] = (
                oc * lax.rsqrt(var + 1e-5) * _silu(gate)).astype(y_ref.dtype)


def even_core(proj, w_pool, pool_scale, batch, seq, *, ts=512):
    t = proj.shape[0]
    pw = len(POOL_WINDOWS) * POOL_GROUP_DIM
    rw = RET_HEADS * RET_HEAD_DIM
    assert pw == rw and proj.shape[1] == pw + 4 * rw
    ts = min(ts, seq)
    assert ts % RET_BLOCK == 0 and RET_BLOCK % CHUNK == 0 and ts % POOL_HALO == 0
    nsb = seq // ts
    half = RET_HEAD_DIM // 2

    pos = jnp.arange(seq, dtype=F32)
    inv = ROPE_BASE ** (-jnp.arange(half, dtype=F32) / half)
    ang = pos[:, None] * inv[None, :]
    cos, sin = jnp.cos(ang), jnp.sin(ang)
    log_g = tuple(math.log(1.0 - 2.0 ** (-5.0 - h)) for h in range(RET_HEADS))
    idx = jnp.arange(RET_BLOCK)
    visible = (idx[None, :] // CHUNK) <= (idx[:, None] // CHUNK)
    gap = jnp.abs(idx[:, None] - idx[None, :]).astype(F32)
    dmask = jnp.stack([jnp.where(visible, jnp.exp(lg * gap), 0.0) for lg in log_g]).astype(F32)

    def col(c):
        return pl.BlockSpec((ts, pw), lambda b, s: (b * nsb + s, c))

    halo = pl.BlockSpec(
        (POOL_HALO, pw), lambda b, s: (jnp.maximum((b * nsb + s) * (ts // POOL_HALO) - 1, 0), 0))
    return pl.pallas_call(
        functools.partial(_even_core_kernel, ts=ts, log_g=log_g),
        grid=(batch, nsb),
        in_specs=[
            col(0), halo, col(1), col(2), col(3), col(4),
            pl.BlockSpec((ts, half), lambda b, s: (s, 0)),
            pl.BlockSpec((ts, half), lambda b, s: (s, 0)),
            pl.BlockSpec((RET_HEADS, RET_BLOCK, RET_BLOCK), lambda b, s: (0, 0, 0)),
            pl.BlockSpec(w_pool.shape, lambda b, s: (0, 0, 0)),
            pl.BlockSpec((1, pw), lambda b, s: (0, 0)),
        ],
        out_specs=pl.BlockSpec((ts, pw + rw), lambda b, s: (b * nsb + s, 0)),
        out_shape=jax.ShapeDtypeStruct((t, pw + rw), BF16),
        scratch_shapes=[pltpu.VMEM((RET_HEADS, RET_HEAD_DIM, RET_HEAD_DIM), F32)],
        compiler_params=_params(("arbitrary", "arbitrary"), 48),
        name="even_core",
    )(proj, proj, proj, proj, proj, proj, cos, sin, dmask, w_pool, pool_scale.reshape(1, pw))


def _matmul_residual_kernel(y_ref, w_ref, h_ref, o_ref):
    o_ref[...] = h_ref[...] + jnp.dot(y_ref[...], w_ref[...], preferred_element_type=F32)


def matmul_residual(y, w, layer, h, *, tm=512):
    t, k = y.shape
    d = w.shape[2]
    tm = min(tm, t)
    return pl.pallas_call(
        _matmul_residual_kernel,
        grid=(t // tm,),
        in_specs=[
            pl.BlockSpec((tm, k), lambda i: (i, 0)),
            pl.BlockSpec((None, k, d), lambda i: (layer, 0, 0)),
            pl.BlockSpec((tm, d), lambda i: (i, 0)),
        ],
        out_specs=pl.BlockSpec((tm, d), lambda i: (i, 0)),
        out_shape=jax.ShapeDtypeStruct((t, d), F32),
        compiler_params=_params(("arbitrary",), 48),
        name="matmul_residual",
    )(y, w, h)


def _ffn_kernel(x_ref, g_ref, wg_ref, wu_ref, wd_ref, o_ref, xn_ref):
    @pl.when(pl.program_id(1) == 0)
    def _():
        x = x_ref[...]
        xn_ref[...] = _rms(x, g_ref[...]).astype(xn_ref.dtype)
        o_ref[...] = x

    xn = xn_ref[...]
    gate = jnp.dot(xn, wg_ref[...], preferred_element_type=F32)
    up = jnp.dot(xn, wu_ref[...], preferred_element_type=F32)
    act = (_silu(gate) * up).astype(BF16)
    o_ref[...] += jnp.dot(act, wd_ref[...], preferred_element_type=F32)


def ffn(h, g, w_gu, w_down, layer, *, tm=1024, tf=512):
    t, d = h.shape
    f = w_down.shape[1]
    tm, tf = min(tm, t), min(tf, f)
    nj = f // tf
    return pl.pallas_call(
        _ffn_kernel,
        grid=(t // tm, nj),
        in_specs=[
            pl.BlockSpec((tm, d), lambda i, j: (i, 0)),
            pl.BlockSpec((1, d), lambda i, j: (0, 0)),
            pl.BlockSpec((None, d, tf), lambda i, j: (layer, 0, j)),
            pl.BlockSpec((None, d, tf), lambda i, j: (layer, 0, j + nj)),
            pl.BlockSpec((None, tf, d), lambda i, j: (layer, j, 0)),
        ],
        out_specs=pl.BlockSpec((tm, d), lambda i, j: (i, 0)),
        out_shape=jax.ShapeDtypeStruct((t, d), F32),
        scratch_shapes=[pltpu.VMEM((tm, d), BF16)],
        compiler_params=_params(("arbitrary", "arbitrary"), 56),
        name="ffn",
    )(h, g.reshape(1, d), w_gu, w_gu, w_down)


def _conv_out_kernel(bg_ref, cg_ref, hx_ref, cgh_ref, hxh_ref, cw_ref, w_ref, h_ref, o_ref, *, ts):
    s = pl.program_id(1)
    z = cg_ref[...].astype(F32) * hx_ref[...].astype(F32)
    zh = cgh_ref[...].astype(F32) * hxh_ref[...].astype(F32)
    zh = jnp.where(s > 0, zh, 0.0)
    prev1 = zh[CONV_HALO - 1:CONV_HALO, :]
    prev2 = zh[CONV_HALO - 2:CONV_HALO - 1, :]
    row = lax.broadcasted_iota(jnp.int32, (ts, 1), 0)
    z1 = jnp.where(row == 0, prev1, pltpu.roll(z, 1, 0))
    z2 = jnp.where(row == 0, prev2, jnp.where(row == 1, prev1, pltpu.roll(z, 2, 0)))
    cw = cw_ref[...]
    conv = z2 * cw[0:1, :] + z1 * cw[1:2, :] + z * cw[2:3, :]
    act = (bg_ref[...].astype(F32) * conv).astype(BF16)
    o_ref[...] = h_ref[...] + jnp.dot(act, w_ref[...], preferred_element_type=F32)


def conv_out(proj, conv_w, w_out, layer, h, batch, seq, *, ts=512):
    t, d = h.shape
    assert proj.shape[1] == 3 * d and conv_w.shape[0] == 3
    ts = min(ts, seq)
    nsb = seq // ts

    def col(c):
        return pl.BlockSpec((ts, d), lambda b, s: (b * nsb + s, c))

    def halo(c):
        return pl.BlockSpec(
            (CONV_HALO, d), lambda b, s: (jnp.maximum((b * nsb + s) * (ts // CONV_HALO) - 1, 0), c))

    return pl.pallas_call(
        functools.partial(_conv_out_kernel, ts=ts),
        grid=(batch, nsb),
        in_specs=[
            col(0), col(1), col(2), halo(1), halo(2),
            pl.BlockSpec(conv_w.shape, lambda b, s: (0, 0)),
            pl.BlockSpec((None, d, d), lambda b, s: (layer, 0, 0)),
            pl.BlockSpec((ts, d), lambda b, s: (b * nsb + s, 0)),
        ],
        out_specs=pl.BlockSpec((ts, d), lambda b, s: (b * nsb + s, 0)),
        out_shape=jax.ShapeDtypeStruct((t, d), F32),
        compiler_params=_params(("arbitrary", "arbitrary"), 56),
        name="conv_out",
    )(proj, proj, proj, proj, proj, conv_w, w_out, h)


def _pack_bf16_pairs(x):
    n = x.shape[1] // 2
    hi = lax.bitcast_convert_type(x[:, :n].astype(BF16).astype(F32), jnp.uint32)
    lo = lax.bitcast_convert_type(x[:, n:].astype(BF16).astype(F32), jnp.uint32)
    return hi | (lo >> 16)


def _unpack_pairs_f32(w):
    hi = lax.bitcast_convert_type(w & jnp.uint32(0xFFFF0000), F32)
    lo = lax.bitcast_convert_type(w << 16, F32)
    return hi, lo


def _store_rows_as_tiles(ref, packed):
    rows = packed.shape[0]
    sub = packed.shape[1] // LANES
    for sl in range(sub):
        ref[pl.ds(sl, rows, stride=sub), :] = packed[:, sl * LANES:(sl + 1) * LANES]


def _load_tile_rows(ref, sl, rows, sub):
    return ref[pl.ds(sl, rows, stride=sub), :]


def _router_kernel(x_ref, g_ref, r_ref, xn_ref, idx_ref, wt_ref):
    xn = _rms(x_ref[...], g_ref[...])
    _store_rows_as_tiles(xn_ref, _pack_bf16_pairs(xn))
    xh = xn.astype(BF16)
    xl = (xn - xh.astype(F32)).astype(BF16)
    ph = jnp.dot(xh, r_ref[...], preferred_element_type=F32)
    pl_ = jnp.dot(xl, r_ref[...], preferred_element_type=F32)
    logits = (ph[:, :ROUTER_LANES] + ph[:, ROUTER_LANES:]) + (pl_[:, :ROUTER_LANES] + pl_[:, ROUTER_LANES:])
    lt = logits.T[:N_EXPERTS, :]
    e_id = lax.broadcasted_iota(jnp.int32, lt.shape, 0)
    m1 = jnp.max(lt, axis=0, keepdims=True)
    i1 = jnp.min(jnp.where(lt == m1, e_id, N_EXPERTS), axis=0, keepdims=True)
    rest = jnp.where(e_id == i1, -jnp.inf, lt)
    m2 = jnp.max(rest, axis=0, keepdims=True)
    i2 = jnp.min(jnp.where(rest == m2, e_id, N_EXPERTS), axis=0, keepdims=True)
    ex = jnp.exp(m2 - m1)
    w1 = 1.0 / (1.0 + ex)
    w2 = ex / (1.0 + ex)
    idx_ref[...] = jnp.where(e_id == 0, i1, jnp.where(e_id == 1, i2, 0))
    wt_ref[...] = jnp.where(e_id == 0, w1, jnp.where(e_id == 1, w2, 0.0))


def router(h, g, w_router, *, tm=512):
    t, d = h.shape
    e = w_router.shape[1]
    assert e == N_EXPERTS
    tm = min(tm, t)
    sub = d // 2 // LANES
    r_pad = jnp.zeros((d, ROUTER_LANES), F32).at[:, :e].set(w_router)
    r_hi = r_pad.astype(BF16)
    r_lo = (r_pad - r_hi.astype(F32)).astype(BF16)
    r_split = jnp.concatenate([r_hi, r_lo], axis=1)
    return pl.pallas_call(
        _router_kernel,
        grid=(t // tm,),
        in_specs=[
            pl.BlockSpec((tm, d), lambda i: (i, 0)),
            pl.BlockSpec((1, d), lambda i: (0, 0)),
            pl.BlockSpec((d, 2 * ROUTER_LANES), lambda i: (0, 0)),
        ],
        out_specs=[
            pl.BlockSpec((tm * sub, LANES), lambda i: (i, 0)),
            pl.BlockSpec((e, tm), lambda i: (0, i)),
            pl.BlockSpec((e, tm), lambda i: (0, i)),
        ],
        out_shape=[
            jax.ShapeDtypeStruct((t * sub, LANES), jnp.uint32),
            jax.ShapeDtypeStruct((e, t), jnp.int32),
            jax.ShapeDtypeStruct((e, t), F32),
        ],
        compiler_params=_params(("arbitrary",), 48),
        name="router",
    )(h, g.reshape(1, d), r_split)


def _routing_tables(idx, tm):
    k, t = idx.shape
    n_tiles = (k * t) // tm + N_EXPERTS
    flat = idx.reshape(-1)
    onehot = (flat[:, None] == jnp.arange(N_EXPERTS)[None, :]).astype(jnp.int32)
    csum = jnp.cumsum(onehot, axis=0)
    rank = jnp.sum(onehot * (csum - 1), axis=1)
    counts = csum[-1]
    padded = ((counts + tm - 1) // tm) * tm
    g_end = jnp.cumsum(padded)
    g_start = g_end - padded
    pos = jnp.sum(onehot * g_start[None, :], axis=1) + rank
    n_active = (g_end[-1] // tm).astype(jnp.int32)
    tile_start = jnp.arange(n_tiles, dtype=jnp.int32) * tm
    tile_expert = jnp.sum((tile_start[:, None] >= g_end[None, :]).astype(jnp.int32), axis=1)
    tile_expert = jnp.minimum(tile_expert, N_EXPERTS - 1)
    tile_rows = jnp.clip((g_start + counts)[tile_expert] - tile_start, 0, tm)
    is_active = jnp.arange(n_tiles) < n_active
    tile_rows = jnp.where(is_active, tile_rows, 0).astype(jnp.int32)
    last = tile_expert[n_active - 1]
    tile_expert = jnp.where(is_active, tile_expert, last).astype(jnp.int32)
    return pos.reshape(k, t).astype(jnp.int32), tile_expert, tile_rows, n_active.reshape(1)


def _dispatch_kernel(p0_ref, p1_ref, x_ref, init_hbm, o_hbm, sem, *, tt, sub):
    del init_hbm
    base = pl.program_id(0) * tt

    def body(r, carry):
        src = x_ref.at[pl.ds(pl.multiple_of(r * sub, sub), sub)]
        for p_ref in (p0_ref, p1_ref):
            dst = pl.ds(pl.multiple_of(p_ref[base + r] * sub, sub), sub)
            pltpu.make_async_copy(src, o_hbm.at[dst], sem).start()
        return carry

    lax.fori_loop(0, tt, body, 0, unroll=GATHER_UNROLL)
    for _ in range(TOP_K):
        pltpu.make_async_copy(x_ref, o_hbm.at[pl.ds(0, tt * sub)], sem).wait()


def moe_dispatch(xn, pos, n_rows, init, *, tt=1024):
    lanes = xn.shape[1]
    t = pos.shape[1]
    sub = xn.shape[0] // t
    tt = min(tt, t)
    grid_spec = pltpu.PrefetchScalarGridSpec(
        num_scalar_prefetch=2,
        grid=(t // tt,),
        in_specs=[
            pl.BlockSpec((tt * sub, lanes), lambda i, p0, p1: (i, 0)),
            pl.BlockSpec(memory_space=pl.ANY),
        ],
        out_specs=pl.BlockSpec(memory_space=pl.ANY),
        scratch_shapes=[pltpu.SemaphoreType.DMA(())],
    )
    return pl.pallas_call(
        functools.partial(_dispatch_kernel, tt=tt, sub=sub),
        grid_spec=grid_spec,
        out_shape=jax.ShapeDtypeStruct((n_rows * sub, lanes), jnp.uint32),
        input_output_aliases={3: 0},
        compiler_params=_params(("arbitrary",), 32),
        name="moe_dispatch",
    )(pos[0], pos[1], xn, jnp.zeros((n_rows * sub, lanes), jnp.uint32) if init is None else init)


def _moe_kernel(te_ref, tr_ref, na_ref, x_ref, wg0_ref, wu0_ref, wd0_ref, wg1_ref, wu1_ref, wd1_ref, o_ref,
                xb, acc, *, tm, nj):
    j = pl.program_id(1)
    n_rows = tr_ref[pl.program_id(0)]
    half = xb.shape[1] // 2
    sub = half // LANES
    n_steps = pl.num_programs(1)
    has_second = 2 * j + 1 < nj
    whole = n_rows > tm - MOE_SUB_ROWS

    @pl.when(j == 0)
    def _():
        @pl.when(n_rows > 0)
        def _():
            for sl in range(sub):
                hi, lo = _unpack_pairs_f32(_load_tile_rows(x_ref, sl, tm, sub))
                xb[:, sl * LANES:(sl + 1) * LANES] = hi.astype(BF16)
                xb[:, half + sl * LANES:half + (sl + 1) * LANES] = lo.astype(BF16)

        @pl.when(jnp.logical_not(whole))
        def _():
            acc[...] = jnp.zeros_like(acc)

    def weights(k):
        wg_ref, wu_ref, wd_ref = ((wg0_ref, wu0_ref, wd0_ref), (wg1_ref, wu1_ref, wd1_ref))[k]
        return wg_ref[...].astype(BF16), wu_ref[...].astype(BF16), wd_ref[...].astype(BF16)

    def swiglu_rows(rows, w, assign=False):
        wg, wu, wd = w
        x = xb[rows, :]
        gate = jnp.dot(x, wg, preferred_element_type=F32)
        up = jnp.dot(x, wu, preferred_element_type=F32)
        act = (_silu(gate) * up).astype(BF16)
        y = jnp.dot(act, wd, preferred_element_type=F32)
        if assign:
            acc[rows, :] = y
        else:
            acc[rows, :] += y

    @pl.when(jnp.logical_and(whole, j == 0))
    def _():
        swiglu_rows(slice(None), weights(0), assign=True)
        if nj > 1:
            swiglu_rows(slice(None), weights(1))

    @pl.when(jnp.logical_and(whole, jnp.logical_and(j > 0, has_second)))
    def _():
        swiglu_rows(slice(None), weights(0))
        swiglu_rows(slice(None), weights(1))

    @pl.when(jnp.logical_and(whole, jnp.logical_and(j > 0, jnp.logical_not(has_second))))
    def _():
        swiglu_rows(slice(None), weights(0))

    @pl.when(jnp.logical_and(n_rows > 0, jnp.logical_not(whole)))
    def _():
        for k in range(2):
            @pl.when(jnp.logical_or(k == 0, has_second))
            def _():
                w = weights(k)
                for sb in range(tm // MOE_SUB_ROWS):
                    @pl.when(sb * MOE_SUB_ROWS < n_rows)
                    def _():
                        swiglu_rows(slice(sb * MOE_SUB_ROWS, (sb + 1) * MOE_SUB_ROWS), w)

    @pl.when(j == n_steps - 1)
    def _():
        _store_rows_as_tiles(o_ref, _pack_bf16_pairs(acc[...]))


def moe_experts(xs, tile_expert, tile_rows, n_active, w_gu, w_down, layer, *, tm, tf=256):
    lanes = xs.shape[1]
    d = w_gu.shape[2]
    sub = d // 2 // lanes
    f = w_down.shape[2]
    tf = min(tf, f)
    nj = f // tf
    n_steps = (nj + 1) // 2
    n_tiles = tile_expert.shape[0]
    assert tm % MOE_SUB_ROWS == 0 and xs.shape[0] == n_tiles * tm * sub

    def chunk(k, i, j, na):
        jl = jnp.where(i < na[0], j, n_steps - 1)
        c = 2 * jl + k
        return jnp.where(c < nj, c, max(nj - 2, 0)) if k else c

    def ii(i, na):
        return jnp.minimum(i, na[0] - 1)

    def w_specs(k):
        return [
            pl.BlockSpec((None, None, d, tf), lambda i, j, te, tr, na: (layer, te[i], 0, chunk(k, i, j, na))),
            pl.BlockSpec((None, None, d, tf), lambda i, j, te, tr, na: (layer, te[i], 0, chunk(k, i, j, na) + nj)),
            pl.BlockSpec((None, None, tf, d), lambda i, j, te, tr, na: (layer, te[i], chunk(k, i, j, na), 0)),
        ]

    grid_spec = pltpu.PrefetchScalarGridSpec(
        num_scalar_prefetch=3,
        grid=(n_tiles, n_steps),
        in_specs=[pl.BlockSpec((tm * sub, lanes), lambda i, j, te, tr, na: (ii(i, na), 0))] + w_specs(0) + w_specs(1),
        out_specs=pl.BlockSpec((tm * sub, lanes), lambda i, j, te, tr, na: (i, 0)),
        scratch_shapes=[
            pltpu.VMEM((tm, d), BF16),
            pltpu.VMEM((tm, d), F32),
        ],
    )
    return pl.pallas_call(
        functools.partial(_moe_kernel, tm=tm, nj=nj),
        grid_spec=grid_spec,
        out_shape=jax.ShapeDtypeStruct((n_tiles * tm * sub, lanes), jnp.uint32),
        compiler_params=_params(("arbitrary", "arbitrary"), 60),
        name="moe_experts",
    )(tile_expert, tile_rows, n_active, xs, w_gu, w_gu, w_down, w_gu, w_gu, w_down)


def _combine_kernel(p0_ref, p1_ref, ys_hbm, h_ref, wt_ref, gf_ref, o_ref, buf, sem, *, tc, final_norm):
    i = pl.program_id(0)
    n = pl.num_programs(0)
    half = o_ref.shape[1] // 2
    sub = half // LANES

    def start_gather(tile, slot):
        base = tile * tc

        def body(r, carry):
            dst = pl.ds(pl.multiple_of(r * sub, sub), sub)
            for kk, p_ref in enumerate((p0_ref, p1_ref)):
                src = pl.ds(pl.multiple_of(p_ref[base + r] * sub, sub), sub)
                pltpu.make_async_copy(ys_hbm.at[src], buf.at[slot, kk, dst], sem.at[slot]).start()
            return carry

        lax.fori_loop(0, tc, body, 0, unroll=GATHER_UNROLL)

    @pl.when(i == 0)
    def _():
        start_gather(0, 0)

    slot = i % 2
    for kk in range(TOP_K):
        pltpu.make_async_copy(ys_hbm.at[pl.ds(0, tc * sub)], buf.at[slot, kk], sem.at[slot]).wait()

    @pl.when(i + 1 < n)
    def _():
        start_gather(i + 1, (i + 1) % 2)

    wt = wt_ref[...].T
    w0, w1 = wt[:, 0:1], wt[:, 1:2]
    for sl in range(sub):
        a_hi, a_lo = _unpack_pairs_f32(_load_tile_rows(buf.at[slot, 0], sl, tc, sub))
        b_hi, b_lo = _unpack_pairs_f32(_load_tile_rows(buf.at[slot, 1], sl, tc, sub))
        c_hi = slice(sl * LANES, (sl + 1) * LANES)
        c_lo = slice(half + sl * LANES, half + (sl + 1) * LANES)
        o_ref[:, c_hi] = h_ref[:, c_hi] + (w0 * a_hi + w1 * b_hi)
        o_ref[:, c_lo] = h_ref[:, c_lo] + (w0 * a_lo + w1 * b_lo)
    if final_norm:
        o_ref[...] = _rms(o_ref[...], gf_ref[...])


def moe_combine(ys, pos, wts, h, final_gain, *, tc=256):
    t, d = h.shape
    lanes = ys.shape[1]
    sub = d // 2 // lanes
    tc = min(tc, t)
    final_norm = final_gain is not None
    gf = (final_gain if final_norm else jnp.ones((d,), F32)).reshape(1, d)
    grid_spec = pltpu.PrefetchScalarGridSpec(
        num_scalar_prefetch=2,
        grid=(t // tc,),
        in_specs=[
            pl.BlockSpec(memory_space=pl.ANY),
            pl.BlockSpec((tc, d), lambda i, p0, p1: (i, 0)),
            pl.BlockSpec((N_EXPERTS, tc), lambda i, p0, p1: (0, i)),
            pl.BlockSpec((1, d), lambda i, p0, p1: (0, 0)),
        ],
        out_specs=pl.BlockSpec((tc, d), lambda i, p0, p1: (i, 0)),
        scratch_shapes=[
            pltpu.VMEM((2, TOP_K, tc * sub, lanes), jnp.uint32),
            pltpu.SemaphoreType.DMA((2,)),
        ],
    )
    return pl.pallas_call(
        functools.partial(_combine_kernel, tc=tc, final_norm=final_norm),
        grid_spec=grid_spec,
        out_shape=jax.ShapeDtypeStruct((t, d), F32),
        compiler_params=_params(("arbitrary",), 48),
        name="moe_combine",
    )(pos[0], pos[1], ys, h, wts, gf)


def moe_block(h, g, w_router, w_gu, w_down, layer, final_gain, xs_prev, *, tm=1024):
    t = h.shape[0]
    tm = min(tm, t)
    xn, idx, wts = router(h, g, w_router)
    pos, tile_expert, tile_rows, n_active = _routing_tables(idx[:TOP_K], tm)
    xs = moe_dispatch(xn, pos, tile_expert.shape[0] * tm, xs_prev)
    ys = moe_experts(xs, tile_expert, tile_rows, n_active, w_gu, w_down, layer, tm=tm)
    return moe_combine(ys, pos, wts, h, final_gain), xs


def kernel(x, norm_mix, norm_ffn, norm_final, ev_w_in, ev_pool_w, ev_pool_scale, ev_w_out,
           od_w_in, od_conv_w, od_w_out, ffn_w_gu, ffn_w_down, moe_router, moe_w_gu, moe_w_down):
    batch, seq, d = x.shape
    depth = norm_mix.shape[0]
    h = x.reshape(batch * seq, d)
    ev_w_out16, od_w_out16 = ev_w_out.astype(BF16), od_w_out.astype(BF16)
    ffn_w_gu16, ffn_w_down16 = ffn_w_gu.astype(BF16), ffn_w_down.astype(BF16)
    xs = None
    for layer in range(depth):
        i = layer // 2
        if layer % 2 == 0:
            proj = norm_matmul(h, norm_mix[layer], ev_w_in, i)
            y = even_core(proj, ev_pool_w[i].astype(BF16), ev_pool_scale[i], batch, seq)
            h = matmul_residual(y, ev_w_out16, i, h)
            h = ffn(h, norm_ffn[layer], ffn_w_gu16, ffn_w_down16, i)
        else:
            proj = norm_matmul(h, norm_mix[layer], od_w_in, i)
            h = conv_out(proj, od_conv_w[i], od_w_out16, i, h, batch, seq)
            final_gain = norm_final if layer == depth - 1 else None
            h, xs = moe_block(h, norm_ffn[layer], moe_router[i], moe_w_gu, moe_w_down, i, final_gain, xs)
    return h.reshape(batch, seq, d)
```

```python
import functools
import math

import jax
import jax.numpy as jnp
from jax import lax
from jax.experimental import pallas as pl
from jax.experimental.pallas import tpu as pltpu

EPS = 1e-6
CHUNK = 64
POOL_WINDOWS = (2, 4, 8, 16)
POOL_GROUP_DIM = 256
RET_HEADS = 4
RET_HEAD_DIM = 256
ROPE_BASE = 10000.0
N_EXPERTS = 8
TOP_K = 2

RET_BLOCK = 256
POOL_HALO = 128
CONV_HALO = 16
ROUTER_LANES = 128
GATHER_UNROLL = 8
LANES = 128
MOE_SUB_ROWS = 256

BF16 = jnp.bfloat16
F32 = jnp.float32
MIB = 1024 * 1024


def _params(semantics, vmem_mib):
    return pltpu.CompilerParams(dimension_semantics=semantics, vmem_limit_bytes=vmem_mib * MIB)


def _rms(x, g):
    ms = jnp.mean(x * x, axis=-1, keepdims=True)
    return x * lax.rsqrt(ms + EPS) * g


def _silu(x):
    return x / (1.0 + jnp.exp(-x))


def _norm_matmul_kernel(x_ref, g_ref, w_ref, o_ref, xn_ref):
    j = pl.program_id(1)

    @pl.when(j == 0)
    def _():
        xn = _rms(x_ref[...], g_ref[...]).astype(xn_ref.dtype)
        xn_ref[...] = xn
        o_ref[...] = jnp.dot(xn, w_ref[...].astype(BF16), preferred_element_type=F32).astype(o_ref.dtype)

    @pl.when(j > 0)
    def _():
        o_ref[...] = jnp.dot(xn_ref[...], w_ref[...].astype(BF16),
                             preferred_element_type=F32).astype(o_ref.dtype)


def norm_matmul(x, g, w, layer, *, tm=1024, tn=1024):
    t, d = x.shape
    n = w.shape[2]
    tm, tn = min(tm, t), min(tn, n)
    return pl.pallas_call(
        _norm_matmul_kernel,
        grid=(t // tm, n // tn),
        in_specs=[
            pl.BlockSpec((tm, d), lambda i, j: (i, 0)),
            pl.BlockSpec((1, d), lambda i, j: (0, 0)),
            pl.BlockSpec((None, d, tn), lambda i, j: (layer, 0, j)),
        ],
        out_specs=pl.BlockSpec((tm, tn), lambda i, j: (i, j)),
        out_shape=jax.ShapeDtypeStruct((t, n), BF16),
        scratch_shapes=[pltpu.VMEM((tm, d), BF16)],
        compiler_params=_params(("arbitrary", "arbitrary"), 48),
        name="norm_matmul",
    )(x, g.reshape(1, d), w)


def _even_core_kernel(u_ref, uh_ref, q_ref, k_ref, v_ref, g_ref, cos_ref, sin_ref, dm_ref,
                      wp_ref, ps_ref, y_ref, st_ref, *, ts, log_g):
    s = pl.program_id(1)
    gd = POOL_GROUP_DIM
    pw = len(POOL_WINDOWS) * gd

    @pl.when(s == 0)
    def _():
        st_ref[...] = jnp.zeros_like(st_ref)

    row = lax.broadcasted_iota(jnp.int32, (ts, ts), 0)
    col = lax.broadcasted_iota(jnp.int32, (ts, ts), 1)
    dist = row - col
    hrow = lax.broadcasted_iota(jnp.int32, (ts, POOL_HALO), 0)
    hcol = lax.broadcasted_iota(jnp.int32, (ts, POOL_HALO), 1)
    hdist = hrow - hcol + POOL_HALO
    t_seq = s * ts + lax.broadcasted_iota(jnp.int32, (ts, 1), 0)
    uh = jnp.where(s > 0, uh_ref[...], jnp.zeros_like(uh_ref))
    for g, w in enumerate(POOL_WINDOWS):
        cs = slice(g * gd, (g + 1) * gd)
        ug = u_ref[:, cs]
        band = jnp.logical_and(dist >= 0, dist < w).astype(BF16)
        hband = (hdist < w).astype(BF16)
        wsum = (jnp.dot(band, ug, preferred_element_type=F32)
                + jnp.dot(hband, uh[:, cs], preferred_element_type=F32))
        count = jnp.minimum(t_seq + 1, w).astype(F32)
        p = (wsum / count - ug.astype(F32)).astype(BF16)
        yg = jnp.dot(p, wp_ref[g], preferred_element_type=F32) * ps_ref[:, cs]
        y_ref[:, cs] = yg.astype(y_ref.dtype)

    lb = RET_BLOCK
    hd2 = RET_HEAD_DIM // 2
    n_idx = lax.broadcasted_iota(jnp.int32, (lb, 1), 0).astype(F32)
    for hd in range(RET_HEADS):
        lg = log_g[hd]
        q_decay = jnp.exp(lg * (n_idx + 1.0))
        k_decay = jnp.exp(lg * (lb - 1.0 - n_idx))
        block_decay = math.exp(lg * lb)
        c0 = hd * RET_HEAD_DIM
        for r in range(ts // lb):
            rows = slice(r * lb, (r + 1) * lb)
            cs_, sn_ = cos_ref[rows, :], sin_ref[rows, :]

            def rope(ref):
                x1 = ref[rows, c0:c0 + hd2].astype(F32)
                x2 = ref[rows, c0 + hd2:c0 + 2 * hd2].astype(F32)
                return jnp.concatenate([x1 * cs_ - x2 * sn_, x2 * cs_ + x1 * sn_], axis=-1)

            q = rope(q_ref) * (RET_HEAD_DIM ** -0.5)
            k = rope(k_ref)
            v = v_ref[rows, c0:c0 + RET_HEAD_DIM]
            scores = lax.dot_general(q.astype(BF16), k.astype(BF16), (((1,), (1,)), ((), ())),
                                     preferred_element_type=F32) * dm_ref[hd]
            state = st_ref[hd]
            o = (jnp.dot(scores.astype(BF16), v, preferred_element_type=F32)
                 + jnp.dot((q * q_decay).astype(BF16), state.astype(BF16), preferred_element_type=F32))
            st_ref[hd] = state * block_decay + lax.dot_general(
                (k * k_decay).astype(BF16), v, (((0,), (0,)), ((), ())), preferred_element_type=F32)
            mu = jnp.mean(o, axis=-1, keepdims=True)
            oc = o - mu
            var = jnp.mean(oc * oc, axis=-1, keepdims=True)
            gate = g_ref[rows, c0:c0 + RET_HEAD_DIM].astype(F32)
            y_ref[rows, pw + c0:pw + c0 + RET_HEAD_DIM] = (
                oc * lax.rsqrt(var + 1e-5) * _silu(gate)).astype(y_ref.dtype)


def even_core(proj, w_pool, pool_scale, batch, seq, *, ts=512):
    t = proj.shape[0]
    pw = len(POOL_WINDOWS) * POOL_GROUP_DIM
    rw = RET_HEADS * RET_HEAD_DIM
    assert pw == rw and proj.shape[1] == pw + 4 * rw
    ts = min(ts, seq)
    assert ts % RET_BLOCK == 0 and RET_BLOCK % CHUNK == 0 and ts % POOL_HALO == 0
    nsb = seq // ts
    half = RET_HEAD_DIM // 2

    pos = jnp.arange(seq, dtype=F32)
    inv = ROPE_BASE ** (-jnp.arange(half, dtype=F32) / half)
    ang = pos[:, None] * inv[None, :]
    cos, sin = jnp.cos(ang), jnp.sin(ang)
    log_g = tuple(math.log(1.0 - 2.0 ** (-5.0 - h)) for h in range(RET_HEADS))
    idx = jnp.arange(RET_BLOCK)
    visible = (idx[None, :] // CHUNK) <= (idx[:, None] // CHUNK)
    gap = jnp.abs(idx[:, None] - idx[None, :]).astype(F32)
    dmask = jnp.stack([jnp.where(visible, jnp.exp(lg * gap), 0.0) for lg in log_g]).astype(F32)

    def col(c):
        return pl.BlockSpec((ts, pw), lambda b, s: (b * nsb + s, c))

    halo = pl.BlockSpec(
        (POOL_HALO, pw), lambda b, s: (jnp.maximum((b * nsb + s) * (ts // POOL_HALO) - 1, 0), 0))
    return pl.pallas_call(
        functools.partial(_even_core_kernel, ts=ts, log_g=log_g),
        grid=(batch, nsb),
        in_specs=[
            col(0), halo, col(1), col(2), col(3), col(4),
            pl.BlockSpec((ts, half), lambda b, s: (s, 0)),
            pl.BlockSpec((ts, half), lambda b, s: (s, 0)),
            pl.BlockSpec((RET_HEADS, RET_BLOCK, RET_BLOCK), lambda b, s: (0, 0, 0)),
            pl.BlockSpec(w_pool.shape, lambda b, s: (0, 0, 0)),
            pl.BlockSpec((1, pw), lambda b, s: (0, 0)),
        ],
        out_specs=pl.BlockSpec((ts, pw + rw), lambda b, s: (b * nsb + s, 0)),
        out_shape=jax.ShapeDtypeStruct((t, pw + rw), BF16),
        scratch_shapes=[pltpu.VMEM((RET_HEADS, RET_HEAD_DIM, RET_HEAD_DIM), F32)],
        compiler_params=_params(("arbitrary", "arbitrary"), 48),
        name="even_core",
    )(proj, proj, proj, proj, proj, proj, cos, sin, dmask, w_pool, pool_scale.reshape(1, pw))


def _matmul_residual_kernel(y_ref, w_ref, h_ref, o_ref):
    o_ref[...] = h_ref[...] + jnp.dot(y_ref[...], w_ref[...], preferred_element_type=F32)


def matmul_residual(y, w, layer, h, *, tm=512):
    t, k = y.shape
    d = w.shape[2]
    tm = min(tm, t)
    return pl.pallas_call(
        _matmul_residual_kernel,
        grid=(t // tm,),
        in_specs=[
            pl.BlockSpec((tm, k), lambda i: (i, 0)),
            pl.BlockSpec((None, k, d), lambda i: (layer, 0, 0)),
            pl.BlockSpec((tm, d), lambda i: (i, 0)),
        ],
        out_specs=pl.BlockSpec((tm, d), lambda i: (i, 0)),
        out_shape=jax.ShapeDtypeStruct((t, d), F32),
        compiler_params=_params(("arbitrary",), 48),
        name="matmul_residual",
    )(y, w, h)


def _ffn_kernel(x_ref, g_ref, wg_ref, wu_ref, wd_ref, o_ref, xn_ref):
    @pl.when(pl.program_id(1) == 0)
    def _():
        x = x_ref[...]
        xn_ref[...] = _rms(x, g_ref[...]).astype(xn_ref.dtype)
        o_ref[...] = x

    xn = xn_ref[...]
    gate = jnp.dot(xn, wg_ref[...], preferred_element_type=F32)
    up = jnp.dot(xn, wu_ref[...], preferred_element_type=F32)
    act = (_silu(gate) * up).astype(BF16)
    o_ref[...] += jnp.dot(act, wd_ref[...], preferred_element_type=F32)


def ffn(h, g, w_gu, w_down, layer, *, tm=1024, tf=512):
    t, d = h.shape
    f = w_down.shape[1]
    tm, tf = min(tm, t), min(tf, f)
    nj = f // tf
    return pl.pallas_call(
        _ffn_kernel,
        grid=(t // tm, nj),
        in_specs=[
            pl.BlockSpec((tm, d), lambda i, j: (i, 0)),
            pl.BlockSpec((1, d), lambda i, j: (0, 0)),
            pl.BlockSpec((None, d, tf), lambda i, j: (layer, 0, j)),
            pl.BlockSpec((None, d, tf), lambda i, j: (layer, 0, j + nj)),
            pl.BlockSpec((None, tf, d), lambda i, j: (layer, j, 0)),
        ],
        out_specs=pl.BlockSpec((tm, d), lambda i, j: (i, 0)),
        out_shape=jax.ShapeDtypeStruct((t, d), F32),
        scratch_shapes=[pltpu.VMEM((tm, d), BF16)],
        compiler_params=_params(("arbitrary", "arbitrary"), 56),
        name="ffn",
    )(h, g.reshape(1, d), w_gu, w_gu, w_down)


def _conv_out_kernel(bg_ref, cg_ref, hx_ref, cgh_ref, hxh_ref, cw_ref, w_ref, h_ref, o_ref, *, ts):
    s = pl.program_id(1)
    z = cg_ref[...].astype(F32) * hx_ref[...].astype(F32)
    zh = cgh_ref[...].astype(F32) * hxh_ref[...].astype(F32)
    zh = jnp.where(s > 0, zh, 0.0)
    prev1 = zh[CONV_HALO - 1:CONV_HALO, :]
    prev2 = zh[CONV_HALO - 2:CONV_HALO - 1, :]
    row = lax.broadcasted_iota(jnp.int32, (ts, 1), 0)
    z1 = jnp.where(row == 0, prev1, pltpu.roll(z, 1, 0))
    z2 = jnp.where(row == 0, prev2, jnp.where(row == 1, prev1, pltpu.roll(z, 2, 0)))
    cw = cw_ref[...]
    conv = z2 * cw[0:1, :] + z1 * cw[1:2, :] + z * cw[2:3, :]
    act = (bg_ref[...].astype(F32) * conv).astype(BF16)
    o_ref[...] = h_ref[...] + jnp.dot(act, w_ref[...], preferred_element_type=F32)


def conv_out(proj, conv_w, w_out, layer, h, batch, seq, *, ts=512):
    t, d = h.shape
    assert proj.shape[1] == 3 * d and conv_w.shape[0] == 3
    ts = min(ts, seq)
    nsb = seq // ts

    def col(c):
        return pl.BlockSpec((ts, d), lambda b, s: (b * nsb + s, c))

    def halo(c):
        return pl.BlockSpec(
            (CONV_HALO, d), lambda b, s: (jnp.maximum((b * nsb + s) * (ts // CONV_HALO) - 1, 0), c))

    return pl.pallas_call(
        functools.partial(_conv_out_kernel, ts=ts),
        grid=(batch, nsb),
        in_specs=[
            col(0), col(1), col(2), halo(1), halo(2),
            pl.BlockSpec(conv_w.shape, lambda b, s: (0, 0)),
            pl.BlockSpec((None, d, d), lambda b, s: (layer, 0, 0)),
            pl.BlockSpec((ts, d), lambda b, s: (b * nsb + s, 0)),
        ],
        out_specs=pl.BlockSpec((ts, d), lambda b, s: (b * nsb + s, 0)),
        out_shape=jax.ShapeDtypeStruct((t, d), F32),
        compiler_params=_params(("arbitrary", "arbitrary"), 56),
        name="conv_out",
    )(proj, proj, proj, proj, proj, conv_w, w_out, h)


def _pack_bf16_pairs(x):
    n = x.shape[1] // 2
    hi = lax.bitcast_convert_type(x[:, :n].astype(BF16).astype(F32), jnp.uint32)
    lo = lax.bitcast_convert_type(x[:, n:].astype(BF16).astype(F32), jnp.uint32)
    return hi | (lo >> 16)


def _unpack_pairs_f32(w):
    hi = lax.bitcast_convert_type(w & jnp.uint32(0xFFFF0000), F32)
    lo = lax.bitcast_convert_type(w << 16, F32)
    return hi, lo


def _store_rows_as_tiles(ref, packed):
    rows = packed.shape[0]
    sub = packed.shape[1] // LANES
    for sl in range(sub):
        ref[pl.ds(sl, rows, stride=sub), :] = packed[:, sl * LANES:(sl + 1) * LANES]


def _load_tile_rows(ref, sl, rows, sub):
    return ref[pl.ds(sl, rows, stride=sub), :]


def _router_kernel(x_ref, g_ref, r_ref, xn_ref, idx_ref, wt_ref):
    xn = _rms(x_ref[...], g_ref[...])
    _store_rows_as_tiles(xn_ref, _pack_bf16_pairs(xn))
    xh = xn.astype(BF16)
    xl = (xn - xh.astype(F32)).astype(BF16)
    ph = jnp.dot(xh, r_ref[...], preferred_element_type=F32)
    pl_ = jnp.dot(xl, r_ref[...], preferred_element_type=F32)
    logits = (ph[:, :ROUTER_LANES] + ph[:, ROUTER_LANES:]) + (pl_[:, :ROUTER_LANES] + pl_[:, ROUTER_LANES:])
    lt = logits.T[:N_EXPERTS, :]
    e_id = lax.broadcasted_iota(jnp.int32, lt.shape, 0)
    m1 = jnp.max(lt, axis=0, keepdims=True)
    i1 = jnp.min(jnp.where(lt == m1, e_id, N_EXPERTS), axis=0, keepdims=True)
    rest = jnp.where(e_id == i1, -jnp.inf, lt)
    m2 = jnp.max(rest, axis=0, keepdims=True)
    i2 = jnp.min(jnp.where(rest == m2, e_id, N_EXPERTS), axis=0, keepdims=True)
    ex = jnp.exp(m2 - m1)
    w1 = 1.0 / (1.0 + ex)
    w2 = ex / (1.0 + ex)
    idx_ref[...] = jnp.where(e_id == 0, i1, jnp.where(e_id == 1, i2, 0))
    wt_ref[...] = jnp.where(e_id == 0, w1, jnp.where(e_id == 1, w2, 0.0))


def router(h, g, w_router, *, tm=512):
    t, d = h.shape
    e = w_router.shape[1]
    assert e == N_EXPERTS
    tm = min(tm, t)
    sub = d // 2 // LANES
    r_pad = jnp.zeros((d, ROUTER_LANES), F32).at[:, :e].set(w_router)
    r_hi = r_pad.astype(BF16)
    r_lo = (r_pad - r_hi.astype(F32)).astype(BF16)
    r_split = jnp.concatenate([r_hi, r_lo], axis=1)
    return pl.pallas_call(
        _router_kernel,
        grid=(t // tm,),
        in_specs=[
            pl.BlockSpec((tm, d), lambda i: (i, 0)),
            pl.BlockSpec((1, d), lambda i: (0, 0)),
            pl.BlockSpec((d, 2 * ROUTER_LANES), lambda i: (0, 0)),
        ],
        out_specs=[
            pl.BlockSpec((tm * sub, LANES), lambda i: (i, 0)),
            pl.BlockSpec((e, tm), lambda i: (0, i)),
            pl.BlockSpec((e, tm), lambda i: (0, i)),
        ],
        out_shape=[
            jax.ShapeDtypeStruct((t * sub, LANES), jnp.uint32),
            jax.ShapeDtypeStruct((e, t), jnp.int32),
            jax.ShapeDtypeStruct((e, t), F32),
        ],
        compiler_params=_params(("arbitrary",), 48),
        name="router",
    )(h, g.reshape(1, d), r_split)


def _routing_tables(idx, tm):
    k, t = idx.shape
    n_tiles = (k * t) // tm + N_EXPERTS
    flat = idx.reshape(-1)
    onehot = (flat[:, None] == jnp.arange(N_EXPERTS)[None, :]).astype(jnp.int32)
    csum = jnp.cumsum(onehot, axis=0)
    rank = jnp.sum(onehot * (csum - 1), axis=1)
    counts = csum[-1]
    padded = ((counts + tm - 1) // tm) * tm
    g_end = jnp.cumsum(padded)
    g_start = g_end - padded
    pos = jnp.sum(onehot * g_start[None, :], axis=1) + rank
    n_active = (g_end[-1] // tm).astype(jnp.int32)
    tile_start = jnp.arange(n_tiles, dtype=jnp.int32) * tm
    tile_expert = jnp.sum((tile_start[:, None] >= g_end[None, :]).astype(jnp.int32), axis=1)
    tile_expert = jnp.minimum(tile_expert, N_EXPERTS - 1)
    tile_rows = jnp.clip((g_start + counts)[tile_expert] - tile_start, 0, tm)
    is_active = jnp.arange(n_tiles) < n_active
    tile_rows = jnp.where(is_active, tile_rows, 0).astype(jnp.int32)
    last = tile_expert[n_active - 1]
    tile_expert = jnp.where(is_active, tile_expert, last).astype(jnp.int32)
    return pos.reshape(k, t).astype(jnp.int32), tile_expert, tile_rows, n_active.reshape(1)


def _dispatch_kernel(p0_ref, p1_ref, x_ref, init_hbm, o_hbm, sem, *, tt, sub):
    del init_hbm
    base = pl.program_id(0) * tt

    def body(r, carry):
        src = x_ref.at[pl.ds(pl.multiple_of(r * sub, sub), sub)]
        for p_ref in (p0_ref, p1_ref):
            dst = pl.ds(pl.multiple_of(p_ref[base + r] * sub, sub), sub)
            pltpu.make_async_copy(src, o_hbm.at[dst], sem).start()
        return carry

    lax.fori_loop(0, tt, body, 0, unroll=GATHER_UNROLL)
    for _ in range(TOP_K):
        pltpu.make_async_copy(x_ref, o_hbm.at[pl.ds(0, tt * sub)], sem).wait()


def moe_dispatch(xn, pos, n_rows, init, *, tt=1024):
    lanes = xn.shape[1]
    t = pos.shape[1]
    sub = xn.shape[0] // t
    tt = min(tt, t)
    grid_spec = pltpu.PrefetchScalarGridSpec(
        num_scalar_prefetch=2,
        grid=(t // tt,),
        in_specs=[
            pl.BlockSpec((tt * sub, lanes), lambda i, p0, p1: (i, 0)),
            pl.BlockSpec(memory_space=pl.ANY),
        ],
        out_specs=pl.BlockSpec(memory_space=pl.ANY),
        scratch_shapes=[pltpu.SemaphoreType.DMA(())],
    )
    return pl.pallas_call(
        functools.partial(_dispatch_kernel, tt=tt, sub=sub),
        grid_spec=grid_spec,
        out_shape=jax.ShapeDtypeStruct((n_rows * sub, lanes), jnp.uint32),
        input_output_aliases={3: 0},
        compiler_params=_params(("arbitrary",), 32),
        name="moe_dispatch",
    )(pos[0], pos[1], xn, jnp.zeros((n_rows * sub, lanes), jnp.uint32) if init is None else init)


def _moe_kernel(te_ref, tr_ref, na_ref, x_ref, wg0_ref, wu0_ref, wd0_ref, wg1_ref, wu1_ref, wd1_ref, o_ref,
                xb, acc, *, tm, nj):
    j = pl.program_id(1)
    n_rows = tr_ref[pl.program_id(0)]
    half = xb.shape[1] // 2
    sub = half // LANES
    n_steps = pl.num_programs(1)
    has_second = 2 * j + 1 < nj
    whole = n_rows > tm - MOE_SUB_ROWS

    def unpack_rows():
        pieces = [_unpack_pairs_f32(_load_tile_rows(x_ref, sl, tm, sub)) for sl in range(sub)]
        return jnp.concatenate([p[0].astype(BF16) for p in pieces] + [p[1].astype(BF16) for p in pieces], axis=1)

    @pl.when(jnp.logical_and(j == 0, jnp.logical_not(whole)))
    def _():
        @pl.when(n_rows > 0)
        def _():
            xb[...] = unpack_rows()

        acc[...] = jnp.zeros_like(acc)

    def weights(k):
        wg_ref, wu_ref, wd_ref = ((wg0_ref, wu0_ref, wd0_ref), (wg1_ref, wu1_ref, wd1_ref))[k]
        return wg_ref[...].astype(BF16), wu_ref[...].astype(BF16), wd_ref[...].astype(BF16)

    def swiglu_rows(rows, w, assign=False, x=None):
        wg, wu, wd = w
        x = xb[rows, :] if x is None else x
        gate = jnp.dot(x, wg, preferred_element_type=F32)
        up = jnp.dot(x, wu, preferred_element_type=F32)
        act = (_silu(gate) * up).astype(BF16)
        y = jnp.dot(act, wd, preferred_element_type=F32)
        if assign:
            acc[rows, :] = y
        else:
            acc[rows, :] += y

    @pl.when(jnp.logical_and(whole, j == 0))
    def _():
        x = unpack_rows()
        xb[...] = x
        swiglu_rows(slice(None), weights(0), assign=True, x=x)
        if nj > 1:
            swiglu_rows(slice(None), weights(1), x=x)

    @pl.when(jnp.logical_and(whole, jnp.logical_and(j > 0, has_second)))
    def _():
        swiglu_rows(slice(None), weights(0))
        swiglu_rows(slice(None), weights(1))

    @pl.when(jnp.logical_and(whole, jnp.logical_and(j > 0, jnp.logical_not(has_second))))
    def _():
        swiglu_rows(slice(None), weights(0))

    @pl.when(jnp.logical_and(n_rows > 0, jnp.logical_not(whole)))
    def _():
        for k in range(2):
            @pl.when(jnp.logical_or(k == 0, has_second))
            def _():
                w = weights(k)
                for sb in range(tm // MOE_SUB_ROWS):
                    @pl.when(sb * MOE_SUB_ROWS < n_rows)
                    def _():
                        swiglu_rows(slice(sb * MOE_SUB_ROWS, (sb + 1) * MOE_SUB_ROWS), w)

    @pl.when(j == n_steps - 1)
    def _():
        _store_rows_as_tiles(o_ref, _pack_bf16_pairs(acc[...]))


def moe_experts(xs, tile_expert, tile_rows, n_active, w_gu, w_down, layer, *, tm, tf=256):
    lanes = xs.shape[1]
    d = w_gu.shape[2]
    sub = d // 2 // lanes
    f = w_down.shape[2]
    tf = min(tf, f)
    nj = f // tf
    n_steps = (nj + 1) // 2
    n_tiles = tile_expert.shape[0]
    assert tm % MOE_SUB_ROWS == 0 and xs.shape[0] == n_tiles * tm * sub

    def chunk(k, i, j, na):
        jl = jnp.where(i < na[0], j, n_steps - 1)
        c = 2 * jl + k
        return jnp.where(c < nj, c, max(nj - 2, 0)) if k else c

    def ii(i, na):
        return jnp.minimum(i, na[0] - 1)

    def w_specs(k):
        return [
            pl.BlockSpec((None, None, d, tf), lambda i, j, te, tr, na: (layer, te[i], 0, chunk(k, i, j, na))),
            pl.BlockSpec((None, None, d, tf), lambda i, j, te, tr, na: (layer, te[i], 0, chunk(k, i, j, na) + nj)),
            pl.BlockSpec((None, None, tf, d), lambda i, j, te, tr, na: (layer, te[i], chunk(k, i, j, na), 0)),
        ]

    grid_spec = pltpu.PrefetchScalarGridSpec(
        num_scalar_prefetch=3,
        grid=(n_tiles, n_steps),
        in_specs=[pl.BlockSpec((tm * sub, lanes), lambda i, j, te, tr, na: (ii(i, na), 0))] + w_specs(0) + w_specs(1),
        out_specs=pl.BlockSpec((tm * sub, lanes), lambda i, j, te, tr, na: (i, 0)),
        scratch_shapes=[
            pltpu.VMEM((tm, d), BF16),
            pltpu.VMEM((tm, d), F32),
        ],
    )
    return pl.pallas_call(
        functools.partial(_moe_kernel, tm=tm, nj=nj),
        grid_spec=grid_spec,
        out_shape=jax.ShapeDtypeStruct((n_tiles * tm * sub, lanes), jnp.uint32),
        compiler_params=_params(("arbitrary", "arbitrary"), 60),
        name="moe_experts",
    )(tile_expert, tile_rows, n_active, xs, w_gu, w_gu, w_down, w_gu, w_gu, w_down)


def _combine_kernel(p0_ref, p1_ref, ys_hbm, h_ref, wt_ref, gf_ref, o_ref, buf, sem, *, tc, final_norm):
    i = pl.program_id(0)
    n = pl.num_programs(0)
    half = o_ref.shape[1] // 2
    sub = half // LANES

    def start_gather(tile, slot):
        base = tile * tc

        def body(r, carry):
            dst = pl.ds(pl.multiple_of(r * sub, sub), sub)
            for kk, p_ref in enumerate((p0_ref, p1_ref)):
                src = pl.ds(pl.multiple_of(p_ref[base + r] * sub, sub), sub)
                pltpu.make_async_copy(ys_hbm.at[src], buf.at[slot, kk, dst], sem.at[slot]).start()
            return carry

        lax.fori_loop(0, tc, body, 0, unroll=GATHER_UNROLL)

    @pl.when(i == 0)
    def _():
        start_gather(0, 0)

    slot = i % 2
    for kk in range(TOP_K):
        pltpu.make_async_copy(ys_hbm.at[pl.ds(0, tc * sub)], buf.at[slot, kk], sem.at[slot]).wait()

    @pl.when(i + 1 < n)
    def _():
        start_gather(i + 1, (i + 1) % 2)

    wt = wt_ref[...].T
    w0, w1 = wt[:, 0:1], wt[:, 1:2]
    for sl in range(sub):
        a_hi, a_lo = _unpack_pairs_f32(_load_tile_rows(buf.at[slot, 0], sl, tc, sub))
        b_hi, b_lo = _unpack_pairs_f32(_load_tile_rows(buf.at[slot, 1], sl, tc, sub))
        c_hi = slice(sl * LANES, (sl + 1) * LANES)
        c_lo = slice(half + sl * LANES, half + (sl + 1) * LANES)
        o_ref[:, c_hi] = h_ref[:, c_hi] + (w0 * a_hi + w1 * b_hi)
        o_ref[:, c_lo] = h_ref[:, c_lo] + (w0 * a_lo + w1 * b_lo)
    if final_norm:
        o_ref[...] = _rms(o_ref[...], gf_ref[...])


def moe_combine(ys, pos, wts, h, final_gain, *, tc=256):
    t, d = h.shape
    lanes = ys.shape[1]
    sub = d // 2 // lanes
    tc = min(tc, t)
    final_norm = final_gain is not None
    gf = (final_gain if final_norm else jnp.ones((d,), F32)).reshape(1, d)
    grid_spec = pltpu.PrefetchScalarGridSpec(
        num_scalar_prefetch=2,
        grid=(t // tc,),
        in_specs=[
            pl.BlockSpec(memory_space=pl.ANY),
            pl.BlockSpec((tc, d), lambda i, p0, p1: (i, 0)),
            pl.BlockSpec((N_EXPERTS, tc), lambda i, p0, p1: (0, i)),
            pl.BlockSpec((1, d), lambda i, p0, p1: (0, 0)),
        ],
        out_specs=pl.BlockSpec((tc, d), lambda i, p0, p1: (i, 0)),
        scratch_shapes=[
            pltpu.VMEM((2, TOP_K, tc * sub, lanes), jnp.uint32),
            pltpu.SemaphoreType.DMA((2,)),
        ],
    )
    return pl.pallas_call(
        functools.partial(_combine_kernel, tc=tc, final_norm=final_norm),
        grid_spec=grid_spec,
        out_shape=jax.ShapeDtypeStruct((t, d), F32),
        compiler_params=_params(("arbitrary",), 48),
        name="moe_combine",
    )(pos[0], pos[1], ys, h, wts, gf)


def moe_block(h, g, w_router, w_gu, w_down, layer, final_gain, xs_prev, *, tm=1024):
    t = h.shape[0]
    tm = min(tm, t)
    xn, idx, wts = router(h, g, w_router)
    pos, tile_expert, tile_rows, n_active = _routing_tables(idx[:TOP_K], tm)
    xs = moe_dispatch(xn, pos, tile_expert.shape[0] * tm, xs_prev)
    ys = moe_experts(xs, tile_expert, tile_rows, n_active, w_gu, w_down, layer, tm=tm)
    return moe_combine(ys, pos, wts, h, final_gain), xs


def kernel(x, norm_mix, norm_ffn, norm_final, ev_w_in, ev_pool_w, ev_pool_scale, ev_w_out,
           od_w_in, od_conv_w, od_w_out, ffn_w_gu, ffn_w_down, moe_router, moe_w_gu, moe_w_down):
    batch, seq, d = x.shape
    depth = norm_mix.shape[0]
    h = x.reshape(batch * seq, d)
    ev_w_out16, od_w_out16 = ev_w_out.astype(BF16), od_w_out.astype(BF16)
    ffn_w_gu16, ffn_w_down16 = ffn_w_gu.astype(BF16), ffn_w_down.astype(BF16)
    xs = None
    for layer in range(depth):
        i = layer // 2
        if layer % 2 == 0:
            proj = norm_matmul(h, norm_mix[layer], ev_w_in, i)
            y = even_core(proj, ev_pool_w[i].astype(BF16), ev_pool_scale[i], batch, seq)
            h = matmul_residual(y, ev_w_out16, i, h)
            h = ffn(h, norm_ffn[layer], ffn_w_gu16, ffn_w_down16, i)
        else:
            proj = norm_matmul(h, norm_mix[layer], od_w_in, i)
            h = conv_out(proj, od_conv_w[i], od_w_out16, i, h, batch, seq)
            final_gain = norm_final if layer == depth - 1 else None
            h, xs = moe_block(h, norm_ffn[layer], moe_router[i], moe_w_gu, moe_w_down, i, final_gain, xs)
    return h.reshape(batch, seq, d)
```

```python
import functools
import math

import jax
import jax.numpy as jnp
from jax import lax
from jax.experimental import pallas as pl
from jax.experimental.pallas import tpu as pltpu

EPS = 1e-6
CHUNK = 64
POOL_WINDOWS = (2, 4, 8, 16)
POOL_GROUP_DIM = 256
RET_HEADS = 4
RET_HEAD_DIM = 256
ROPE_BASE = 10000.0
N_EXPERTS = 8
TOP_K = 2

RET_BLOCK = 256
POOL_HALO = 128
CONV_HALO = 16
ROUTER_LANES = 128
GATHER_UNROLL = 8
LANES = 128
MOE_SUB_ROWS = 256

BF16 = jnp.bfloat16
F32 = jnp.float32
MIB = 1024 * 1024


def _params(semantics, vmem_mib):
    return pltpu.CompilerParams(dimension_semantics=semantics, vmem_limit_bytes=vmem_mib * MIB)


def _rms(x, g):
    ms = jnp.mean(x * x, axis=-1, keepdims=True)
    return x * lax.rsqrt(ms + EPS) * g


def _silu(x):
    return x / (1.0 + jnp.exp(-x))


def _norm_matmul_kernel(x_ref, g_ref, w_ref, o_ref, xn_ref):
    j = pl.program_id(1)

    @pl.when(j == 0)
    def _():
        xn = _rms(x_ref[...], g_ref[...]).astype(xn_ref.dtype)
        xn_ref[...] = xn
        o_ref[...] = jnp.dot(xn, w_ref[...].astype(BF16), preferred_element_type=F32).astype(o_ref.dtype)

    @pl.when(j > 0)
    def _():
        o_ref[...] = jnp.dot(xn_ref[...], w_ref[...].astype(BF16),
                             preferred_element_type=F32).astype(o_ref.dtype)


def norm_matmul(x, g, w, layer, *, tm=1024, tn=1024):
    t, d = x.shape
    n = w.shape[2]
    tm, tn = min(tm, t), min(tn, n)
    return pl.pallas_call(
        _norm_matmul_kernel,
        grid=(t // tm, n // tn),
        in_specs=[
            pl.BlockSpec((tm, d), lambda i, j: (i, 0)),
            pl.BlockSpec((1, d), lambda i, j: (0, 0)),
            pl.BlockSpec((None, d, tn), lambda i, j: (layer, 0, j)),
        ],
        out_specs=pl.BlockSpec((tm, tn), lambda i, j: (i, j)),
        out_shape=jax.ShapeDtypeStruct((t, n), BF16),
        scratch_shapes=[pltpu.VMEM((tm, d), BF16)],
        compiler_params=_params(("arbitrary", "arbitrary"), 48),
        name="norm_matmul",
    )(x, g.reshape(1, d), w)


def _even_core_kernel(u_ref, uh_ref, q_ref, k_ref, v_ref, g_ref, cos_ref, sin_ref, dm_ref,
                      wp_ref, ps_ref, y_ref, st_ref, *, ts, log_g):
    s = pl.program_id(1)
    gd = POOL_GROUP_DIM
    pw = len(POOL_WINDOWS) * gd

    @pl.when(s == 0)
    def _():
        st_ref[...] = jnp.zeros_like(st_ref)

    row = lax.broadcasted_iota(jnp.int32, (ts, ts), 0)
    col = lax.broadcasted_iota(jnp.int32, (ts, ts), 1)
    dist = row - col
    hrow = lax.broadcasted_iota(jnp.int32, (ts, POOL_HALO), 0)
    hcol = lax.broadcasted_iota(jnp.int32, (ts, POOL_HALO), 1)
    hdist = hrow - hcol + POOL_HALO
    t_seq = s * ts + lax.broadcasted_iota(jnp.int32, (ts, 1), 0)
    uh = jnp.where(s > 0, uh_ref[...], jnp.zeros_like(uh_ref))
    for g, w in enumerate(POOL_WINDOWS):
        cs = slice(g * gd, (g + 1) * gd)
        ug = u_ref[:, cs]
        band = jnp.logical_and(dist >= 0, dist < w).astype(BF16)
        hband = (hdist < w).astype(BF16)
        wsum = (jnp.dot(band, ug, preferred_element_type=F32)
                + jnp.dot(hband, uh[:, cs], preferred_element_type=F32))
        count = jnp.minimum(t_seq + 1, w).astype(F32)
        p = (wsum / count - ug.astype(F32)).astype(BF16)
        yg = jnp.dot(p, wp_ref[g], preferred_element_type=F32) * ps_ref[:, cs]
        y_ref[:, cs] = yg.astype(y_ref.dtype)

    lb = RET_BLOCK
    hd2 = RET_HEAD_DIM // 2
    n_idx = lax.broadcasted_iota(jnp.int32, (lb, 1), 0).astype(F32)
    for hd in range(RET_HEADS):
        lg = log_g[hd]
        q_decay = jnp.exp(lg * (n_idx + 1.0))
        k_decay = jnp.exp(lg * (lb - 1.0 - n_idx))
        block_decay = math.exp(lg * lb)
        c0 = hd * RET_HEAD_DIM
        for r in range(ts // lb):
            rows = slice(r * lb, (r + 1) * lb)
            cs_, sn_ = cos_ref[rows, :], sin_ref[rows, :]

            def rope(ref):
                x1 = ref[rows, c0:c0 + hd2].astype(F32)
                x2 = ref[rows, c0 + hd2:c0 + 2 * hd2].astype(F32)
                return jnp.concatenate([x1 * cs_ - x2 * sn_, x2 * cs_ + x1 * sn_], axis=-1)

            q = rope(q_ref) * (RET_HEAD_DIM ** -0.5)
            k = rope(k_ref)
            v = v_ref[rows, c0:c0 + RET_HEAD_DIM]
            scores = lax.dot_general(q.astype(BF16), k.astype(BF16), (((1,), (1,)), ((), ())),
                                     preferred_element_type=F32) * dm_ref[hd]
            state = st_ref[hd]
            o = (jnp.dot(scores.astype(BF16), v, preferred_element_type=F32)
                 + jnp.dot((q * q_decay).astype(BF16), state.astype(BF16), preferred_element_type=F32))
            st_ref[hd] = state * block_decay + lax.dot_general(
                (k * k_decay).astype(BF16), v, (((0,), (0,)), ((), ())), preferred_element_type=F32)
            mu = jnp.mean(o, axis=-1, keepdims=True)
            oc = o - mu
            var = jnp.mean(oc * oc, axis=-1, keepdims=True)
            gate = g_ref[rows, c0:c0 + RET_HEAD_DIM].astype(F32)
            y_ref[rows, pw + c0:pw + c0 + RET_HEAD_DIM] = (
                oc * lax.rsqrt(var + 1e-5) * _silu(gate)).astype(y_ref.dtype)


def even_core(proj, w_pool, pool_scale, batch, seq, *, ts=512):
    t = proj.shape[0]
    pw = len(POOL_WINDOWS) * POOL_GROUP_DIM
    rw = RET_HEADS * RET_HEAD_DIM
    assert pw == rw and proj.shape[1] == pw + 4 * rw
    ts = min(ts, seq)
    assert ts % RET_BLOCK == 0 and RET_BLOCK % CHUNK == 0 and ts % POOL_HALO == 0
    nsb = seq // ts
    half = RET_HEAD_DIM // 2

    pos = jnp.arange(seq, dtype=F32)
    inv = ROPE_BASE ** (-jnp.arange(half, dtype=F32) / half)
    ang = pos[:, None] * inv[None, :]
    cos, sin = jnp.cos(ang), jnp.sin(ang)
    log_g = tuple(math.log(1.0 - 2.0 ** (-5.0 - h)) for h in range(RET_HEADS))
    idx = jnp.arange(RET_BLOCK)
    visible = (idx[None, :] // CHUNK) <= (idx[:, None] // CHUNK)
    gap = jnp.abs(idx[:, None] - idx[None, :]).astype(F32)
    dmask = jnp.stack([jnp.where(visible, jnp.exp(lg * gap), 0.0) for lg in log_g]).astype(F32)

    def col(c):
        return pl.BlockSpec((ts, pw), lambda b, s: (b * nsb + s, c))

    halo = pl.BlockSpec(
        (POOL_HALO, pw), lambda b, s: (jnp.maximum((b * nsb + s) * (ts // POOL_HALO) - 1, 0), 0))
    return pl.pallas_call(
        functools.partial(_even_core_kernel, ts=ts, log_g=log_g),
        grid=(batch, nsb),
        in_specs=[
            col(0), halo, col(1), col(2), col(3), col(4),
            pl.BlockSpec((ts, half), lambda b, s: (s, 0)),
            pl.BlockSpec((ts, half), lambda b, s: (s, 0)),
            pl.BlockSpec((RET_HEADS, RET_BLOCK, RET_BLOCK), lambda b, s: (0, 0, 0)),
            pl.BlockSpec(w_pool.shape, lambda b, s: (0, 0, 0)),
            pl.BlockSpec((1, pw), lambda b, s: (0, 0)),
        ],
        out_specs=pl.BlockSpec((ts, pw + rw), lambda b, s: (b * nsb + s, 0)),
        out_shape=jax.ShapeDtypeStruct((t, pw + rw), BF16),
        scratch_shapes=[pltpu.VMEM((RET_HEADS, RET_HEAD_DIM, RET_HEAD_DIM), F32)],
        compiler_params=_params(("arbitrary", "arbitrary"), 48),
        name="even_core",
    )(proj, proj, proj, proj, proj, proj, cos, sin, dmask, w_pool, pool_scale.reshape(1, pw))


def _matmul_residual_kernel(y_ref, w_ref, h_ref, o_ref):
    o_ref[...] = h_ref[...] + jnp.dot(y_ref[...], w_ref[...], preferred_element_type=F32)


def matmul_residual(y, w, layer, h, *, tm=512):
    t, k = y.shape
    d = w.shape[2]
    tm = min(tm, t)
    return pl.pallas_call(
        _matmul_residual_kernel,
        grid=(t // tm,),
        in_specs=[
            pl.BlockSpec((tm, k), lambda i: (i, 0)),
            pl.BlockSpec((None, k, d), lambda i: (layer, 0, 0)),
            pl.BlockSpec((tm, d), lambda i: (i, 0)),
        ],
        out_specs=pl.BlockSpec((tm, d), lambda i: (i, 0)),
        out_shape=jax.ShapeDtypeStruct((t, d), F32),
        compiler_params=_params(("arbitrary",), 48),
        name="matmul_residual",
    )(y, w, h)


def _ffn_kernel(x_ref, g_ref, wg_ref, wu_ref, wd_ref, o_ref, xn_ref):
    j = pl.program_id(1)

    def swiglu(xn):
        gate = jnp.dot(xn, wg_ref[...], preferred_element_type=F32)
        up = jnp.dot(xn, wu_ref[...], preferred_element_type=F32)
        act = (_silu(gate) * up).astype(BF16)
        return jnp.dot(act, wd_ref[...], preferred_element_type=F32)

    @pl.when(j == 0)
    def _():
        x = x_ref[...]
        xn = _rms(x, g_ref[...]).astype(xn_ref.dtype)
        xn_ref[...] = xn
        o_ref[...] = x + swiglu(xn)

    @pl.when(j > 0)
    def _():
        o_ref[...] += swiglu(xn_ref[...])


def ffn(h, g, w_gu, w_down, layer, *, tm=1024, tf=512):
    t, d = h.shape
    f = w_down.shape[1]
    tm, tf = min(tm, t), min(tf, f)
    nj = f // tf
    return pl.pallas_call(
        _ffn_kernel,
        grid=(t // tm, nj),
        in_specs=[
            pl.BlockSpec((tm, d), lambda i, j: (i, 0)),
            pl.BlockSpec((1, d), lambda i, j: (0, 0)),
            pl.BlockSpec((None, d, tf), lambda i, j: (layer, 0, j)),
            pl.BlockSpec((None, d, tf), lambda i, j: (layer, 0, j + nj)),
            pl.BlockSpec((None, tf, d), lambda i, j: (layer, j, 0)),
        ],
        out_specs=pl.BlockSpec((tm, d), lambda i, j: (i, 0)),
        out_shape=jax.ShapeDtypeStruct((t, d), F32),
        scratch_shapes=[pltpu.VMEM((tm, d), BF16)],
        compiler_params=_params(("arbitrary", "arbitrary"), 56),
        name="ffn",
    )(h, g.reshape(1, d), w_gu, w_gu, w_down)


def _conv_out_kernel(bg_ref, cg_ref, hx_ref, cgh_ref, hxh_ref, cw_ref, w_ref, h_ref, o_ref, *, ts):
    s = pl.program_id(1)
    z = cg_ref[...].astype(F32) * hx_ref[...].astype(F32)
    zh = cgh_ref[...].astype(F32) * hxh_ref[...].astype(F32)
    zh = jnp.where(s > 0, zh, 0.0)
    prev1 = zh[CONV_HALO - 1:CONV_HALO, :]
    prev2 = zh[CONV_HALO - 2:CONV_HALO - 1, :]
    row = lax.broadcasted_iota(jnp.int32, (ts, 1), 0)
    z1 = jnp.where(row == 0, prev1, pltpu.roll(z, 1, 0))
    z2 = jnp.where(row == 0, prev2, jnp.where(row == 1, prev1, pltpu.roll(z, 2, 0)))
    cw = cw_ref[...]
    conv = z2 * cw[0:1, :] + z1 * cw[1:2, :] + z * cw[2:3, :]
    act = (bg_ref[...].astype(F32) * conv).astype(BF16)
    o_ref[...] = h_ref[...] + jnp.dot(act, w_ref[...], preferred_element_type=F32)


def conv_out(proj, conv_w, w_out, layer, h, batch, seq, *, ts=512):
    t, d = h.shape
    assert proj.shape[1] == 3 * d and conv_w.shape[0] == 3
    ts = min(ts, seq)
    nsb = seq // ts

    def col(c):
        return pl.BlockSpec((ts, d), lambda b, s: (b * nsb + s, c))

    def halo(c):
        return pl.BlockSpec(
            (CONV_HALO, d), lambda b, s: (jnp.maximum((b * nsb + s) * (ts // CONV_HALO) - 1, 0), c))

    return pl.pallas_call(
        functools.partial(_conv_out_kernel, ts=ts),
        grid=(batch, nsb),
        in_specs=[
            col(0), col(1), col(2), halo(1), halo(2),
            pl.BlockSpec(conv_w.shape, lambda b, s: (0, 0)),
            pl.BlockSpec((None, d, d), lambda b, s: (layer, 0, 0)),
            pl.BlockSpec((ts, d), lambda b, s: (b * nsb + s, 0)),
        ],
        out_specs=pl.BlockSpec((ts, d), lambda b, s: (b * nsb + s, 0)),
        out_shape=jax.ShapeDtypeStruct((t, d), F32),
        compiler_params=_params(("arbitrary", "arbitrary"), 56),
        name="conv_out",
    )(proj, proj, proj, proj, proj, conv_w, w_out, h)


def _pack_bf16_pairs(x):
    n = x.shape[1] // 2
    hi = lax.bitcast_convert_type(x[:, :n].astype(BF16).astype(F32), jnp.uint32)
    lo = lax.bitcast_convert_type(x[:, n:].astype(BF16).astype(F32), jnp.uint32)
    return hi | (lo >> 16)


def _unpack_pairs_f32(w):
    hi = lax.bitcast_convert_type(w & jnp.uint32(0xFFFF0000), F32)
    lo = lax.bitcast_convert_type(w << 16, F32)
    return hi, lo


def _store_rows_as_tiles(ref, packed):
    rows = packed.shape[0]
    sub = packed.shape[1] // LANES
    for sl in range(sub):
        ref[pl.ds(sl, rows, stride=sub), :] = packed[:, sl * LANES:(sl + 1) * LANES]


def _load_tile_rows(ref, sl, rows, sub):
    return ref[pl.ds(sl, rows, stride=sub), :]


def _router_kernel(x_ref, g_ref, r_ref, xn_ref, idx_ref, wt_ref):
    xn = _rms(x_ref[...], g_ref[...])
    _store_rows_as_tiles(xn_ref, _pack_bf16_pairs(xn))
    xh = xn.astype(BF16)
    xl = (xn - xh.astype(F32)).astype(BF16)
    ph = jnp.dot(xh, r_ref[...], preferred_element_type=F32)
    pl_ = jnp.dot(xl, r_ref[...], preferred_element_type=F32)
    logits = (ph[:, :ROUTER_LANES] + ph[:, ROUTER_LANES:]) + (pl_[:, :ROUTER_LANES] + pl_[:, ROUTER_LANES:])
    lt = logits.T[:N_EXPERTS, :]
    e_id = lax.broadcasted_iota(jnp.int32, lt.shape, 0)
    m1 = jnp.max(lt, axis=0, keepdims=True)
    i1 = jnp.min(jnp.where(lt == m1, e_id, N_EXPERTS), axis=0, keepdims=True)
    rest = jnp.where(e_id == i1, -jnp.inf, lt)
    m2 = jnp.max(rest, axis=0, keepdims=True)
    i2 = jnp.min(jnp.where(rest == m2, e_id, N_EXPERTS), axis=0, keepdims=True)
    ex = jnp.exp(m2 - m1)
    w1 = 1.0 / (1.0 + ex)
    w2 = ex / (1.0 + ex)
    idx_ref[...] = jnp.where(e_id == 0, i1, jnp.where(e_id == 1, i2, 0))
    wt_ref[...] = jnp.where(e_id == 0, w1, jnp.where(e_id == 1, w2, 0.0))


def router(h, g, w_router, *, tm=512):
    t, d = h.shape
    e = w_router.shape[1]
    assert e == N_EXPERTS
    tm = min(tm, t)
    sub = d // 2 // LANES
    r_pad = jnp.zeros((d, ROUTER_LANES), F32).at[:, :e].set(w_router)
    r_hi = r_pad.astype(BF16)
    r_lo = (r_pad - r_hi.astype(F32)).astype(BF16)
    r_split = jnp.concatenate([r_hi, r_lo], axis=1)
    return pl.pallas_call(
        _router_kernel,
        grid=(t // tm,),
        in_specs=[
            pl.BlockSpec((tm, d), lambda i: (i, 0)),
            pl.BlockSpec((1, d), lambda i: (0, 0)),
            pl.BlockSpec((d, 2 * ROUTER_LANES), lambda i: (0, 0)),
        ],
        out_specs=[
            pl.BlockSpec((tm * sub, LANES), lambda i: (i, 0)),
            pl.BlockSpec((e, tm), lambda i: (0, i)),
            pl.BlockSpec((e, tm), lambda i: (0, i)),
        ],
        out_shape=[
            jax.ShapeDtypeStruct((t * sub, LANES), jnp.uint32),
            jax.ShapeDtypeStruct((e, t), jnp.int32),
            jax.ShapeDtypeStruct((e, t), F32),
        ],
        compiler_params=_params(("arbitrary",), 48),
        name="router",
    )(h, g.reshape(1, d), r_split)


def _routing_tables(idx, tm):
    k, t = idx.shape
    n_tiles = (k * t) // tm + N_EXPERTS
    flat = idx.reshape(-1)
    onehot = (flat[:, None] == jnp.arange(N_EXPERTS)[None, :]).astype(jnp.int32)
    csum = jnp.cumsum(onehot, axis=0)
    rank = jnp.sum(onehot * (csum - 1), axis=1)
    counts = csum[-1]
    padded = ((counts + tm - 1) // tm) * tm
    g_end = jnp.cumsum(padded)
    g_start = g_end - padded
    pos = jnp.sum(onehot * g_start[None, :], axis=1) + rank
    n_active = (g_end[-1] // tm).astype(jnp.int32)
    tile_start = jnp.arange(n_tiles, dtype=jnp.int32) * tm
    tile_expert = jnp.sum((tile_start[:, None] >= g_end[None, :]).astype(jnp.int32), axis=1)
    tile_expert = jnp.minimum(tile_expert, N_EXPERTS - 1)
    tile_rows = jnp.clip((g_start + counts)[tile_expert] - tile_start, 0, tm)
    is_active = jnp.arange(n_tiles) < n_active
    tile_rows = jnp.where(is_active, tile_rows, 0).astype(jnp.int32)
    last = tile_expert[n_active - 1]
    tile_expert = jnp.where(is_active, tile_expert, last).astype(jnp.int32)
    return pos.reshape(k, t).astype(jnp.int32), tile_expert, tile_rows, n_active.reshape(1)


def _dispatch_kernel(p0_ref, p1_ref, x_ref, init_hbm, o_hbm, sem, *, tt, sub):
    del init_hbm
    base = pl.program_id(0) * tt

    def body(r, carry):
        src = x_ref.at[pl.ds(pl.multiple_of(r * sub, sub), sub)]
        for p_ref in (p0_ref, p1_ref):
            dst = pl.ds(pl.multiple_of(p_ref[base + r] * sub, sub), sub)
            pltpu.make_async_copy(src, o_hbm.at[dst], sem).start()
        return carry

    lax.fori_loop(0, tt, body, 0, unroll=GATHER_UNROLL)
    for _ in range(TOP_K):
        pltpu.make_async_copy(x_ref, o_hbm.at[pl.ds(0, tt * sub)], sem).wait()


def moe_dispatch(xn, pos, n_rows, init, *, tt=1024):
    lanes = xn.shape[1]
    t = pos.shape[1]
    sub = xn.shape[0] // t
    tt = min(tt, t)
    grid_spec = pltpu.PrefetchScalarGridSpec(
        num_scalar_prefetch=2,
        grid=(t // tt,),
        in_specs=[
            pl.BlockSpec((tt * sub, lanes), lambda i, p0, p1: (i, 0)),
            pl.BlockSpec(memory_space=pl.ANY),
        ],
        out_specs=pl.BlockSpec(memory_space=pl.ANY),
        scratch_shapes=[pltpu.SemaphoreType.DMA(())],
    )
    return pl.pallas_call(
        functools.partial(_dispatch_kernel, tt=tt, sub=sub),
        grid_spec=grid_spec,
        out_shape=jax.ShapeDtypeStruct((n_rows * sub, lanes), jnp.uint32),
        input_output_aliases={3: 0},
        compiler_params=_params(("arbitrary",), 32),
        name="moe_dispatch",
    )(pos[0], pos[1], xn, jnp.zeros((n_rows * sub, lanes), jnp.uint32) if init is None else init)


def _moe_kernel(te_ref, tr_ref, na_ref, x_ref, wg0_ref, wu0_ref, wd0_ref, wg1_ref, wu1_ref, wd1_ref, o_ref,
                xb, acc, *, tm, nj):
    j = pl.program_id(1)
    n_rows = tr_ref[pl.program_id(0)]
    half = xb.shape[1] // 2
    sub = half // LANES
    n_steps = (nj + 1) // 2
    has_second = 2 * j + 1 < nj
    whole = n_rows > tm - MOE_SUB_ROWS

    def unpack_rows():
        pieces = [_unpack_pairs_f32(_load_tile_rows(x_ref, sl, tm, sub)) for sl in range(sub)]
        return jnp.concatenate([p[0].astype(BF16) for p in pieces] + [p[1].astype(BF16) for p in pieces], axis=1)

    @pl.when(jnp.logical_and(j == 0, jnp.logical_not(whole)))
    def _():
        @pl.when(n_rows > 0)
        def _():
            xb[...] = unpack_rows()

        acc[...] = jnp.zeros_like(acc)

    def weights(k):
        wg_ref, wu_ref, wd_ref = ((wg0_ref, wu0_ref, wd0_ref), (wg1_ref, wu1_ref, wd1_ref))[k]
        return wg_ref[...].astype(BF16), wu_ref[...].astype(BF16), wd_ref[...].astype(BF16)

    def swiglu_rows(rows, w, assign=False, x=None, final=False):
        wg, wu, wd = w
        x = xb[rows, :] if x is None else x
        gate = jnp.dot(x, wg, preferred_element_type=F32)
        up = jnp.dot(x, wu, preferred_element_type=F32)
        act = (_silu(gate) * up).astype(BF16)
        y = jnp.dot(act, wd, preferred_element_type=F32)
        if final:
            _store_rows_as_tiles(o_ref, _pack_bf16_pairs(y if assign else acc[rows, :] + y))
        elif assign:
            acc[rows, :] = y
        else:
            acc[rows, :] += y

    def whole_step(n_chunks, first, last):
        x = None
        if first:
            x = unpack_rows()
            xb[...] = x
        for k in range(n_chunks):
            swiglu_rows(slice(None), weights(k), assign=first and k == 0, x=x, final=last and k == n_chunks - 1)

    n_last = 1 if nj % 2 else 2
    if n_steps == 1:
        pl.when(whole)(functools.partial(whole_step, n_last, True, True))
    else:
        pl.when(jnp.logical_and(whole, j == 0))(functools.partial(whole_step, 2, True, False))
        if n_steps > 2:
            middle = jnp.logical_and(j > 0, j < n_steps - 1)
            pl.when(jnp.logical_and(whole, middle))(functools.partial(whole_step, 2, False, False))
        pl.when(jnp.logical_and(whole, j == n_steps - 1))(functools.partial(whole_step, n_last, False, True))

    @pl.when(jnp.logical_and(n_rows > 0, jnp.logical_not(whole)))
    def _():
        for k in range(2):
            @pl.when(jnp.logical_or(k == 0, has_second))
            def _():
                w = weights(k)
                for sb in range(tm // MOE_SUB_ROWS):
                    @pl.when(sb * MOE_SUB_ROWS < n_rows)
                    def _():
                        swiglu_rows(slice(sb * MOE_SUB_ROWS, (sb + 1) * MOE_SUB_ROWS), w)

    @pl.when(jnp.logical_and(j == n_steps - 1, jnp.logical_not(whole)))
    def _():
        _store_rows_as_tiles(o_ref, _pack_bf16_pairs(acc[...]))


def moe_experts(xs, tile_expert, tile_rows, n_active, w_gu, w_down, layer, *, tm, tf=256):
    lanes = xs.shape[1]
    d = w_gu.shape[2]
    sub = d // 2 // lanes
    f = w_down.shape[2]
    tf = min(tf, f)
    nj = f // tf
    n_steps = (nj + 1) // 2
    n_tiles = tile_expert.shape[0]
    assert tm % MOE_SUB_ROWS == 0 and xs.shape[0] == n_tiles * tm * sub

    def chunk(k, i, j, na):
        jl = jnp.where(i < na[0], j, n_steps - 1)
        c = 2 * jl + k
        return jnp.where(c < nj, c, max(nj - 2, 0)) if k else c

    def ii(i, na):
        return jnp.minimum(i, na[0] - 1)

    def w_specs(k):
        return [
            pl.BlockSpec((None, None, d, tf), lambda i, j, te, tr, na: (layer, te[i], 0, chunk(k, i, j, na))),
            pl.BlockSpec((None, None, d, tf), lambda i, j, te, tr, na: (layer, te[i], 0, chunk(k, i, j, na) + nj)),
            pl.BlockSpec((None, None, tf, d), lambda i, j, te, tr, na: (layer, te[i], chunk(k, i, j, na), 0)),
        ]

    grid_spec = pltpu.PrefetchScalarGridSpec(
        num_scalar_prefetch=3,
        grid=(n_tiles, n_steps),
        in_specs=[pl.BlockSpec((tm * sub, lanes), lambda i, j, te, tr, na: (ii(i, na), 0))] + w_specs(0) + w_specs(1),
        out_specs=pl.BlockSpec((tm * sub, lanes), lambda i, j, te, tr, na: (i, 0)),
        scratch_shapes=[
            pltpu.VMEM((tm, d), BF16),
            pltpu.VMEM((tm, d), F32),
        ],
    )
    return pl.pallas_call(
        functools.partial(_moe_kernel, tm=tm, nj=nj),
        grid_spec=grid_spec,
        out_shape=jax.ShapeDtypeStruct((n_tiles * tm * sub, lanes), jnp.uint32),
        compiler_params=_params(("arbitrary", "arbitrary"), 60),
        name="moe_experts",
    )(tile_expert, tile_rows, n_active, xs, w_gu, w_gu, w_down, w_gu, w_gu, w_down)


def _combine_kernel(p0_ref, p1_ref, ys_hbm, h_ref, wt_ref, gf_ref, o_ref, buf, sem, *, tc, final_norm):
    i = pl.program_id(0)
    n = pl.num_programs(0)
    half = o_ref.shape[1] // 2
    sub = half // LANES

    def start_gather(tile, slot):
        base = tile * tc

        def body(r, carry):
            dst = pl.ds(pl.multiple_of(r * sub, sub), sub)
            for kk, p_ref in enumerate((p0_ref, p1_ref)):
                src = pl.ds(pl.multiple_of(p_ref[base + r] * sub, sub), sub)
                pltpu.make_async_copy(ys_hbm.at[src], buf.at[slot, kk, dst], sem.at[slot]).start()
            return carry

        lax.fori_loop(0, tc, body, 0, unroll=GATHER_UNROLL)

    @pl.when(i == 0)
    def _():
        start_gather(0, 0)

    slot = i % 2
    for kk in range(TOP_K):
        pltpu.make_async_copy(ys_hbm.at[pl.ds(0, tc * sub)], buf.at[slot, kk], sem.at[slot]).wait()

    @pl.when(i + 1 < n)
    def _():
        start_gather(i + 1, (i + 1) % 2)

    wt = wt_ref[...].T
    w0, w1 = wt[:, 0:1], wt[:, 1:2]
    for sl in range(sub):
        a_hi, a_lo = _unpack_pairs_f32(_load_tile_rows(buf.at[slot, 0], sl, tc, sub))
        b_hi, b_lo = _unpack_pairs_f32(_load_tile_rows(buf.at[slot, 1], sl, tc, sub))
        c_hi = slice(sl * LANES, (sl + 1) * LANES)
        c_lo = slice(half + sl * LANES, half + (sl + 1) * LANES)
        o_ref[:, c_hi] = h_ref[:, c_hi] + (w0 * a_hi + w1 * b_hi)
        o_ref[:, c_lo] = h_ref[:, c_lo] + (w0 * a_lo + w1 * b_lo)
    if final_norm:
        o_ref[...] = _rms(o_ref[...], gf_ref[...])


def moe_combine(ys, pos, wts, h, final_gain, *, tc=256):
    t, d = h.shape
    lanes = ys.shape[1]
    sub = d // 2 // lanes
    tc = min(tc, t)
    final_norm = final_gain is not None
    gf = (final_gain if final_norm else jnp.ones((d,), F32)).reshape(1, d)
    grid_spec = pltpu.PrefetchScalarGridSpec(
        num_scalar_prefetch=2,
        grid=(t // tc,),
        in_specs=[
            pl.BlockSpec(memory_space=pl.ANY),
            pl.BlockSpec((tc, d), lambda i, p0, p1: (i, 0)),
            pl.BlockSpec((N_EXPERTS, tc), lambda i, p0, p1: (0, i)),
            pl.BlockSpec((1, d), lambda i, p0, p1: (0, 0)),
        ],
        out_specs=pl.BlockSpec((tc, d), lambda i, p0, p1: (i, 0)),
        scratch_shapes=[
            pltpu.VMEM((2, TOP_K, tc * sub, lanes), jnp.uint32),
            pltpu.SemaphoreType.DMA((2,)),
        ],
    )
    return pl.pallas_call(
        functools.partial(_combine_kernel, tc=tc, final_norm=final_norm),
        grid_spec=grid_spec,
        out_shape=jax.ShapeDtypeStruct((t, d), F32),
        compiler_params=_params(("arbitrary",), 48),
        name="moe_combine",
    )(pos[0], pos[1], ys, h, wts, gf)


def moe_block(h, g, w_router, w_gu, w_down, layer, final_gain, xs_prev, *, tm=1024):
    t = h.shape[0]
    tm = min(tm, t)
    xn, idx, wts = router(h, g, w_router)
    pos, tile_expert, tile_rows, n_active = _routing_tables(idx[:TOP_K], tm)
    xs = moe_dispatch(xn, pos, tile_expert.shape[0] * tm, xs_prev)
    ys = moe_experts(xs, tile_expert, tile_rows, n_active, w_gu, w_down, layer, tm=tm)
    return moe_combine(ys, pos, wts, h, final_gain), xs


def kernel(x, norm_mix, norm_ffn, norm_final, ev_w_in, ev_pool_w, ev_pool_scale, ev_w_out,
           od_w_in, od_conv_w, od_w_out, ffn_w_gu, ffn_w_down, moe_router, moe_w_gu, moe_w_down):
    batch, seq, d = x.shape
    depth = norm_mix.shape[0]
    h = x.reshape(batch * seq, d)
    ev_w_out16, od_w_out16 = ev_w_out.astype(BF16), od_w_out.astype(BF16)
    ffn_w_gu16, ffn_w_down16 = ffn_w_gu.astype(BF16), ffn_w_down.astype(BF16)
    xs = None
    for layer in range(depth):
        i = layer // 2
        if layer % 2 == 0:
            proj = norm_matmul(h, norm_mix[layer], ev_w_in, i)
            y = even_core(proj, ev_pool_w[i].astype(BF16), ev_pool_scale[i], batch, seq)
            h = matmul_residual(y, ev_w_out16, i, h)
            h = ffn(h, norm_ffn[layer], ffn_w_gu16, ffn_w_down16, i)
        else:
            proj = norm_matmul(h, norm_mix[layer], od_w_in, i)
            h = conv_out(proj, od_conv_w[i], od_w_out16, i, h, batch, seq)
            final_gain = norm_final if layer == depth - 1 else None
            h, xs = moe_block(h, norm_ffn[layer], moe_router[i], moe_w_gu, moe_w_down, i, final_gain, xs)
    return h.reshape(batch, seq, d)
```

```python
import functools
import math

import jax
import jax.numpy as jnp
from jax import lax
from jax.experimental import pallas as pl
from jax.experimental.pallas import tpu as pltpu

EPS = 1e-6
CHUNK = 64
POOL_WINDOWS = (2, 4, 8, 16)
POOL_GROUP_DIM = 256
RET_HEADS = 4
RET_HEAD_DIM = 256
ROPE_BASE = 10000.0
N_EXPERTS = 8
TOP_K = 2

RET_BLOCK = 256
POOL_HALO = 128
CONV_HALO = 16
ROUTER_LANES = 128
GATHER_UNROLL = 8
LANES = 128
MOE_SUB_ROWS = 256

BF16 = jnp.bfloat16
F32 = jnp.float32
MIB = 1024 * 1024


def _params(semantics, vmem_mib):
    return pltpu.CompilerParams(dimension_semantics=semantics, vmem_limit_bytes=vmem_mib * MIB)


def _rms(x, g):
    ms = jnp.mean(x * x, axis=-1, keepdims=True)
    return x * lax.rsqrt(ms + EPS) * g


def _silu(x):
    return x / (1.0 + jnp.exp(-x))


def _norm_matmul_kernel(x_ref, g_ref, w_ref, o_ref, xn_ref):
    j = pl.program_id(1)

    @pl.when(j == 0)
    def _():
        xn = _rms(x_ref[...], g_ref[...]).astype(xn_ref.dtype)
        xn_ref[...] = xn
        o_ref[...] = jnp.dot(xn, w_ref[...], preferred_element_type=F32).astype(o_ref.dtype)

    @pl.when(j > 0)
    def _():
        o_ref[...] = jnp.dot(xn_ref[...], w_ref[...], preferred_element_type=F32).astype(o_ref.dtype)


def norm_matmul(x, g, w, layer, *, tm=1024, tn=1024):
    t, d = x.shape
    n = w.shape[2]
    tm, tn = min(tm, t), min(tn, n)
    return pl.pallas_call(
        _norm_matmul_kernel,
        grid=(t // tm, n // tn),
        in_specs=[
            pl.BlockSpec((tm, d), lambda i, j: (i, 0)),
            pl.BlockSpec((1, d), lambda i, j: (0, 0)),
            pl.BlockSpec((None, d, tn), lambda i, j: (layer, 0, j)),
        ],
        out_specs=pl.BlockSpec((tm, tn), lambda i, j: (i, j)),
        out_shape=jax.ShapeDtypeStruct((t, n), BF16),
        scratch_shapes=[pltpu.VMEM((tm, d), BF16)],
        compiler_params=_params(("arbitrary", "arbitrary"), 48),
        name="norm_matmul",
    )(x, g.reshape(1, d), w)


def _even_core_kernel(u_ref, uh_ref, q_ref, k_ref, v_ref, g_ref, cos_ref, sin_ref, dm_ref,
                      wp_ref, ps_ref, y_ref, st_ref, *, ts, log_g):
    s = pl.program_id(1)
    gd = POOL_GROUP_DIM
    pw = len(POOL_WINDOWS) * gd

    @pl.when(s == 0)
    def _():
        st_ref[...] = jnp.zeros_like(st_ref)

    row = lax.broadcasted_iota(jnp.int32, (ts, ts), 0)
    col = lax.broadcasted_iota(jnp.int32, (ts, ts), 1)
    dist = row - col
    hrow = lax.broadcasted_iota(jnp.int32, (ts, POOL_HALO), 0)
    hcol = lax.broadcasted_iota(jnp.int32, (ts, POOL_HALO), 1)
    hdist = hrow - hcol + POOL_HALO
    t_seq = s * ts + lax.broadcasted_iota(jnp.int32, (ts, 1), 0)
    uh = jnp.where(s > 0, uh_ref[...], jnp.zeros_like(uh_ref))
    for g, w in enumerate(POOL_WINDOWS):
        cs = slice(g * gd, (g + 1) * gd)
        ug = u_ref[:, cs]
        band = jnp.logical_and(dist >= 0, dist < w).astype(BF16)
        hband = (hdist < w).astype(BF16)
        wsum = (jnp.dot(band, ug, preferred_element_type=F32)
                + jnp.dot(hband, uh[:, cs], preferred_element_type=F32))
        count = jnp.minimum(t_seq + 1, w).astype(F32)
        p = (wsum / count - ug.astype(F32)).astype(BF16)
        yg = jnp.dot(p, wp_ref[g], preferred_element_type=F32) * ps_ref[:, cs]
        y_ref[:, cs] = yg.astype(y_ref.dtype)

    lb = RET_BLOCK
    hd2 = RET_HEAD_DIM // 2
    n_idx = lax.broadcasted_iota(jnp.int32, (lb, 1), 0).astype(F32)
    for hd in range(RET_HEADS):
        lg = log_g[hd]
        q_decay = jnp.exp(lg * (n_idx + 1.0))
        k_decay = jnp.exp(lg * (lb - 1.0 - n_idx))
        block_decay = math.exp(lg * lb)
        c0 = hd * RET_HEAD_DIM
        for r in range(ts // lb):
            rows = slice(r * lb, (r + 1) * lb)
            cs_, sn_ = cos_ref[rows, :], sin_ref[rows, :]

            def rope(ref):
                x1 = ref[rows, c0:c0 + hd2].astype(F32)
                x2 = ref[rows, c0 + hd2:c0 + 2 * hd2].astype(F32)
                return jnp.concatenate([x1 * cs_ - x2 * sn_, x2 * cs_ + x1 * sn_], axis=-1)

            q = rope(q_ref) * (RET_HEAD_DIM ** -0.5)
            k = rope(k_ref)
            v = v_ref[rows, c0:c0 + RET_HEAD_DIM]
            scores = lax.dot_general(q.astype(BF16), k.astype(BF16), (((1,), (1,)), ((), ())),
                                     preferred_element_type=F32) * dm_ref[hd]
            state = st_ref[hd]
            o = (jnp.dot(scores.astype(BF16), v, preferred_element_type=F32)
                 + jnp.dot((q * q_decay).astype(BF16), state.astype(BF16), preferred_element_type=F32))
            st_ref[hd] = state * block_decay + lax.dot_general(
                (k * k_decay).astype(BF16), v, (((0,), (0,)), ((), ())), preferred_element_type=F32)
            mu = jnp.mean(o, axis=-1, keepdims=True)
            oc = o - mu
            var = jnp.mean(oc * oc, axis=-1, keepdims=True)
            gate = g_ref[rows, c0:c0 + RET_HEAD_DIM].astype(F32)
            y_ref[rows, pw + c0:pw + c0 + RET_HEAD_DIM] = (
                oc * lax.rsqrt(var + 1e-5) * _silu(gate)).astype(y_ref.dtype)


def even_core(proj, w_pool, pool_scale, batch, seq, *, ts=512):
    t = proj.shape[0]
    pw = len(POOL_WINDOWS) * POOL_GROUP_DIM
    rw = RET_HEADS * RET_HEAD_DIM
    assert pw == rw and proj.shape[1] == pw + 4 * rw
    ts = min(ts, seq)
    assert ts % RET_BLOCK == 0 and RET_BLOCK % CHUNK == 0 and ts % POOL_HALO == 0
    nsb = seq // ts
    half = RET_HEAD_DIM // 2

    pos = jnp.arange(seq, dtype=F32)
    inv = ROPE_BASE ** (-jnp.arange(half, dtype=F32) / half)
    ang = pos[:, None] * inv[None, :]
    cos, sin = jnp.cos(ang), jnp.sin(ang)
    log_g = tuple(math.log(1.0 - 2.0 ** (-5.0 - h)) for h in range(RET_HEADS))
    idx = jnp.arange(RET_BLOCK)
    visible = (idx[None, :] // CHUNK) <= (idx[:, None] // CHUNK)
    gap = jnp.abs(idx[:, None] - idx[None, :]).astype(F32)
    dmask = jnp.stack([jnp.where(visible, jnp.exp(lg * gap), 0.0) for lg in log_g]).astype(F32)

    def col(c):
        return pl.BlockSpec((ts, pw), lambda b, s: (b * nsb + s, c))

    halo = pl.BlockSpec(
        (POOL_HALO, pw), lambda b, s: (jnp.maximum((b * nsb + s) * (ts // POOL_HALO) - 1, 0), 0))
    return pl.pallas_call(
        functools.partial(_even_core_kernel, ts=ts, log_g=log_g),
        grid=(batch, nsb),
        in_specs=[
            col(0), halo, col(1), col(2), col(3), col(4),
            pl.BlockSpec((ts, half), lambda b, s: (s, 0)),
            pl.BlockSpec((ts, half), lambda b, s: (s, 0)),
            pl.BlockSpec((RET_HEADS, RET_BLOCK, RET_BLOCK), lambda b, s: (0, 0, 0)),
            pl.BlockSpec(w_pool.shape, lambda b, s: (0, 0, 0)),
            pl.BlockSpec((1, pw), lambda b, s: (0, 0)),
        ],
        out_specs=pl.BlockSpec((ts, pw + rw), lambda b, s: (b * nsb + s, 0)),
        out_shape=jax.ShapeDtypeStruct((t, pw + rw), BF16),
        scratch_shapes=[pltpu.VMEM((RET_HEADS, RET_HEAD_DIM, RET_HEAD_DIM), F32)],
        compiler_params=_params(("arbitrary", "arbitrary"), 48),
        name="even_core",
    )(proj, proj, proj, proj, proj, proj, cos, sin, dmask, w_pool, pool_scale.reshape(1, pw))


def _matmul_residual_kernel(y_ref, w_ref, h_ref, o_ref):
    o_ref[...] = h_ref[...] + jnp.dot(y_ref[...], w_ref[...], preferred_element_type=F32)


def matmul_residual(y, w, layer, h, *, tm=512):
    t, k = y.shape
    d = w.shape[2]
    tm = min(tm, t)
    return pl.pallas_call(
        _matmul_residual_kernel,
        grid=(t // tm,),
        in_specs=[
            pl.BlockSpec((tm, k), lambda i: (i, 0)),
            pl.BlockSpec((None, k, d), lambda i: (layer, 0, 0)),
            pl.BlockSpec((tm, d), lambda i: (i, 0)),
        ],
        out_specs=pl.BlockSpec((tm, d), lambda i: (i, 0)),
        out_shape=jax.ShapeDtypeStruct((t, d), F32),
        compiler_params=_params(("arbitrary",), 48),
        name="matmul_residual",
    )(y, w, h)


def _ffn_kernel(x_ref, g_ref, wg_ref, wu_ref, wd_ref, o_ref, xn_ref):
    j = pl.program_id(1)

    def swiglu(xn):
        gate = jnp.dot(xn, wg_ref[...], preferred_element_type=F32)
        up = jnp.dot(xn, wu_ref[...], preferred_element_type=F32)
        act = (_silu(gate) * up).astype(BF16)
        return jnp.dot(act, wd_ref[...], preferred_element_type=F32)

    @pl.when(j == 0)
    def _():
        x = x_ref[...]
        xn = _rms(x, g_ref[...]).astype(xn_ref.dtype)
        xn_ref[...] = xn
        o_ref[...] = x + swiglu(xn)

    @pl.when(j > 0)
    def _():
        o_ref[...] += swiglu(xn_ref[...])


def ffn(h, g, w_gu, w_down, layer, *, tm=1024, tf=512):
    t, d = h.shape
    f = w_down.shape[1]
    tm, tf = min(tm, t), min(tf, f)
    nj = f // tf
    return pl.pallas_call(
        _ffn_kernel,
        grid=(t // tm, nj),
        in_specs=[
            pl.BlockSpec((tm, d), lambda i, j: (i, 0)),
            pl.BlockSpec((1, d), lambda i, j: (0, 0)),
            pl.BlockSpec((None, d, tf), lambda i, j: (layer, 0, j)),
            pl.BlockSpec((None, d, tf), lambda i, j: (layer, 0, j + nj)),
            pl.BlockSpec((None, tf, d), lambda i, j: (layer, j, 0)),
        ],
        out_specs=pl.BlockSpec((tm, d), lambda i, j: (i, 0)),
        out_shape=jax.ShapeDtypeStruct((t, d), F32),
        scratch_shapes=[pltpu.VMEM((tm, d), BF16)],
        compiler_params=_params(("arbitrary", "arbitrary"), 56),
        name="ffn",
    )(h, g.reshape(1, d), w_gu, w_gu, w_down)


def _conv_out_kernel(bg_ref, cg_ref, hx_ref, cgh_ref, hxh_ref, cw_ref, w_ref, h_ref, o_ref, *, ts):
    s = pl.program_id(1)
    z = cg_ref[...].astype(F32) * hx_ref[...].astype(F32)
    zh = cgh_ref[...].astype(F32) * hxh_ref[...].astype(F32)
    zh = jnp.where(s > 0, zh, 0.0)
    prev1 = zh[CONV_HALO - 1:CONV_HALO, :]
    prev2 = zh[CONV_HALO - 2:CONV_HALO - 1, :]
    row = lax.broadcasted_iota(jnp.int32, (ts, 1), 0)
    z1 = jnp.where(row == 0, prev1, pltpu.roll(z, 1, 0))
    z2 = jnp.where(row == 0, prev2, jnp.where(row == 1, prev1, pltpu.roll(z, 2, 0)))
    cw = cw_ref[...]
    conv = z2 * cw[0:1, :] + z1 * cw[1:2, :] + z * cw[2:3, :]
    act = (bg_ref[...].astype(F32) * conv).astype(BF16)
    o_ref[...] = h_ref[...] + jnp.dot(act, w_ref[...], preferred_element_type=F32)


def conv_out(proj, conv_w, w_out, layer, h, batch, seq, *, ts=512):
    t, d = h.shape
    assert proj.shape[1] == 3 * d and conv_w.shape[0] == 3
    ts = min(ts, seq)
    nsb = seq // ts

    def col(c):
        return pl.BlockSpec((ts, d), lambda b, s: (b * nsb + s, c))

    def halo(c):
        return pl.BlockSpec(
            (CONV_HALO, d), lambda b, s: (jnp.maximum((b * nsb + s) * (ts // CONV_HALO) - 1, 0), c))

    return pl.pallas_call(
        functools.partial(_conv_out_kernel, ts=ts),
        grid=(batch, nsb),
        in_specs=[
            col(0), col(1), col(2), halo(1), halo(2),
            pl.BlockSpec(conv_w.shape, lambda b, s: (0, 0)),
            pl.BlockSpec((None, d, d), lambda b, s: (layer, 0, 0)),
            pl.BlockSpec((ts, d), lambda b, s: (b * nsb + s, 0)),
        ],
        out_specs=pl.BlockSpec((ts, d), lambda b, s: (b * nsb + s, 0)),
        out_shape=jax.ShapeDtypeStruct((t, d), F32),
        compiler_params=_params(("arbitrary", "arbitrary"), 56),
        name="conv_out",
    )(proj, proj, proj, proj, proj, conv_w, w_out, h)


def _pack_bf16_pairs(x):
    n = x.shape[1] // 2
    hi = lax.bitcast_convert_type(x[:, :n].astype(BF16).astype(F32), jnp.uint32)
    lo = lax.bitcast_convert_type(x[:, n:].astype(BF16).astype(F32), jnp.uint32)
    return hi | (lo >> 16)


def _unpack_pairs_f32(w):
    hi = lax.bitcast_convert_type(w & jnp.uint32(0xFFFF0000), F32)
    lo = lax.bitcast_convert_type(w << 16, F32)
    return hi, lo


def _store_rows_as_tiles(ref, packed):
    rows = packed.shape[0]
    sub = packed.shape[1] // LANES
    for sl in range(sub):
        ref[pl.ds(sl, rows, stride=sub), :] = packed[:, sl * LANES:(sl + 1) * LANES]


def _load_tile_rows(ref, sl, rows, sub):
    return ref[pl.ds(sl, rows, stride=sub), :]


def _router_kernel(x_ref, g_ref, r_ref, xn_ref, idx_ref, wt_ref):
    xn = _rms(x_ref[...], g_ref[...])
    _store_rows_as_tiles(xn_ref, _pack_bf16_pairs(xn))
    xh = xn.astype(BF16)
    xl = (xn - xh.astype(F32)).astype(BF16)
    ph = jnp.dot(xh, r_ref[...], preferred_element_type=F32)
    pl_ = jnp.dot(xl, r_ref[...], preferred_element_type=F32)
    logits = (ph[:, :ROUTER_LANES] + ph[:, ROUTER_LANES:]) + (pl_[:, :ROUTER_LANES] + pl_[:, ROUTER_LANES:])
    lt = logits.T[:N_EXPERTS, :]
    e_id = lax.broadcasted_iota(jnp.int32, lt.shape, 0)
    m1 = jnp.max(lt, axis=0, keepdims=True)
    i1 = jnp.min(jnp.where(lt == m1, e_id, N_EXPERTS), axis=0, keepdims=True)
    rest = jnp.where(e_id == i1, -jnp.inf, lt)
    m2 = jnp.max(rest, axis=0, keepdims=True)
    i2 = jnp.min(jnp.where(rest == m2, e_id, N_EXPERTS), axis=0, keepdims=True)
    ex = jnp.exp(m2 - m1)
    w1 = 1.0 / (1.0 + ex)
    w2 = ex / (1.0 + ex)
    idx_ref[...] = jnp.where(e_id == 0, i1, jnp.where(e_id == 1, i2, 0))
    wt_ref[...] = jnp.where(e_id == 0, w1, jnp.where(e_id == 1, w2, 0.0))


def router(h, g, w_router, *, tm=512):
    t, d = h.shape
    e = w_router.shape[1]
    assert e == N_EXPERTS
    tm = min(tm, t)
    sub = d // 2 // LANES
    r_pad = jnp.zeros((d, ROUTER_LANES), F32).at[:, :e].set(w_router)
    r_hi = r_pad.astype(BF16)
    r_lo = (r_pad - r_hi.astype(F32)).astype(BF16)
    r_split = jnp.concatenate([r_hi, r_lo], axis=1)
    return pl.pallas_call(
        _router_kernel,
        grid=(t // tm,),
        in_specs=[
            pl.BlockSpec((tm, d), lambda i: (i, 0)),
            pl.BlockSpec((1, d), lambda i: (0, 0)),
            pl.BlockSpec((d, 2 * ROUTER_LANES), lambda i: (0, 0)),
        ],
        out_specs=[
            pl.BlockSpec((tm * sub, LANES), lambda i: (i, 0)),
            pl.BlockSpec((e, tm), lambda i: (0, i)),
            pl.BlockSpec((e, tm), lambda i: (0, i)),
        ],
        out_shape=[
            jax.ShapeDtypeStruct((t * sub, LANES), jnp.uint32),
            jax.ShapeDtypeStruct((e, t), jnp.int32),
            jax.ShapeDtypeStruct((e, t), F32),
        ],
        compiler_params=_params(("arbitrary",), 48),
        name="router",
    )(h, g.reshape(1, d), r_split)


def _routing_tables(idx, tm):
    k, t = idx.shape
    n_tiles = (k * t) // tm + N_EXPERTS
    flat = idx.reshape(-1)
    onehot = (flat[:, None] == jnp.arange(N_EXPERTS)[None, :]).astype(jnp.int32)
    csum = jnp.cumsum(onehot, axis=0)
    rank = jnp.sum(onehot * (csum - 1), axis=1)
    counts = csum[-1]
    padded = ((counts + tm - 1) // tm) * tm
    g_end = jnp.cumsum(padded)
    g_start = g_end - padded
    pos = jnp.sum(onehot * g_start[None, :], axis=1) + rank
    n_active = (g_end[-1] // tm).astype(jnp.int32)
    tile_start = jnp.arange(n_tiles, dtype=jnp.int32) * tm
    tile_expert = jnp.sum((tile_start[:, None] >= g_end[None, :]).astype(jnp.int32), axis=1)
    tile_expert = jnp.minimum(tile_expert, N_EXPERTS - 1)
    tile_rows = jnp.clip((g_start + counts)[tile_expert] - tile_start, 0, tm)
    is_active = jnp.arange(n_tiles) < n_active
    tile_rows = jnp.where(is_active, tile_rows, 0).astype(jnp.int32)
    last = tile_expert[n_active - 1]
    tile_expert = jnp.where(is_active, tile_expert, last).astype(jnp.int32)
    return pos.reshape(k, t).astype(jnp.int32), tile_expert, tile_rows, n_active.reshape(1)


def _dispatch_kernel(p0_ref, p1_ref, x_ref, init_hbm, o_hbm, sem, *, tt, sub):
    del init_hbm
    base = pl.program_id(0) * tt

    def body(r, carry):
        src = x_ref.at[pl.ds(pl.multiple_of(r * sub, sub), sub)]
        for p_ref in (p0_ref, p1_ref):
            dst = pl.ds(pl.multiple_of(p_ref[base + r] * sub, sub), sub)
            pltpu.make_async_copy(src, o_hbm.at[dst], sem).start()
        return carry

    lax.fori_loop(0, tt, body, 0, unroll=GATHER_UNROLL)
    for _ in range(TOP_K):
        pltpu.make_async_copy(x_ref, o_hbm.at[pl.ds(0, tt * sub)], sem).wait()


def moe_dispatch(xn, pos, n_rows, init, *, tt=1024):
    lanes = xn.shape[1]
    t = pos.shape[1]
    sub = xn.shape[0] // t
    tt = min(tt, t)
    grid_spec = pltpu.PrefetchScalarGridSpec(
        num_scalar_prefetch=2,
        grid=(t // tt,),
        in_specs=[
            pl.BlockSpec((tt * sub, lanes), lambda i, p0, p1: (i, 0)),
            pl.BlockSpec(memory_space=pl.ANY),
        ],
        out_specs=pl.BlockSpec(memory_space=pl.ANY),
        scratch_shapes=[pltpu.SemaphoreType.DMA(())],
    )
    return pl.pallas_call(
        functools.partial(_dispatch_kernel, tt=tt, sub=sub),
        grid_spec=grid_spec,
        out_shape=jax.ShapeDtypeStruct((n_rows * sub, lanes), jnp.uint32),
        input_output_aliases={3: 0},
        compiler_params=_params(("arbitrary",), 32),
        name="moe_dispatch",
    )(pos[0], pos[1], xn, jnp.zeros((n_rows * sub, lanes), jnp.uint32) if init is None else init)


def _moe_kernel(te_ref, tr_ref, na_ref, x_ref, wg0_ref, wu0_ref, wd0_ref, wg1_ref, wu1_ref, wd1_ref, o_ref,
                xb, acc, *, tm, nj):
    j = pl.program_id(1)
    n_rows = tr_ref[pl.program_id(0)]
    half = xb.shape[1] // 2
    sub = half // LANES
    n_steps = (nj + 1) // 2
    has_second = 2 * j + 1 < nj
    whole = n_rows > tm - MOE_SUB_ROWS

    def unpack_rows():
        pieces = [_unpack_pairs_f32(_load_tile_rows(x_ref, sl, tm, sub)) for sl in range(sub)]
        return jnp.concatenate([p[0].astype(BF16) for p in pieces] + [p[1].astype(BF16) for p in pieces], axis=1)

    @pl.when(jnp.logical_and(j == 0, jnp.logical_not(whole)))
    def _():
        @pl.when(n_rows > 0)
        def _():
            xb[...] = unpack_rows()

        acc[...] = jnp.zeros_like(acc)

    def weights(k):
        wg_ref, wu_ref, wd_ref = ((wg0_ref, wu0_ref, wd0_ref), (wg1_ref, wu1_ref, wd1_ref))[k]
        return wg_ref[...].astype(BF16), wu_ref[...].astype(BF16), wd_ref[...].astype(BF16)

    def swiglu_rows(rows, w, assign=False, x=None, final=False):
        wg, wu, wd = w
        x = xb[rows, :] if x is None else x
        gate = jnp.dot(x, wg, preferred_element_type=F32)
        up = jnp.dot(x, wu, preferred_element_type=F32)
        act = (_silu(gate) * up).astype(BF16)
        y = jnp.dot(act, wd, preferred_element_type=F32)
        if final:
            _store_rows_as_tiles(o_ref, _pack_bf16_pairs(y if assign else acc[rows, :] + y))
        elif assign:
            acc[rows, :] = y
        else:
            acc[rows, :] += y

    def whole_step(n_chunks, first, last):
        x = None
        if first:
            x = unpack_rows()
            xb[...] = x
        w = weights(0)
        if n_chunks == 2:
            (g0, u0, d0), (g1, u1, d1) = w, weights(1)
            w = (jnp.concatenate([g0, g1], axis=1), jnp.concatenate([u0, u1], axis=1),
                 jnp.concatenate([d0, d1], axis=0))
        swiglu_rows(slice(None), w, assign=first, x=x, final=last)

    n_last = 1 if nj % 2 else 2
    if n_steps == 1:
        pl.when(whole)(functools.partial(whole_step, n_last, True, True))
    else:
        pl.when(jnp.logical_and(whole, j == 0))(functools.partial(whole_step, 2, True, False))
        if n_steps > 2:
            middle = jnp.logical_and(j > 0, j < n_steps - 1)
            pl.when(jnp.logical_and(whole, middle))(functools.partial(whole_step, 2, False, False))
        pl.when(jnp.logical_and(whole, j == n_steps - 1))(functools.partial(whole_step, n_last, False, True))

    @pl.when(jnp.logical_and(n_rows > 0, jnp.logical_not(whole)))
    def _():
        for k in range(2):
            @pl.when(jnp.logical_or(k == 0, has_second))
            def _():
                w = weights(k)
                for sb in range(tm // MOE_SUB_ROWS):
                    @pl.when(sb * MOE_SUB_ROWS < n_rows)
                    def _():
                        swiglu_rows(slice(sb * MOE_SUB_ROWS, (sb + 1) * MOE_SUB_ROWS), w)

    @pl.when(jnp.logical_and(j == n_steps - 1, jnp.logical_not(whole)))
    def _():
        _store_rows_as_tiles(o_ref, _pack_bf16_pairs(acc[...]))


def moe_experts(xs, tile_expert, tile_rows, n_active, w_gu, w_down, layer, *, tm, tf=256):
    lanes = xs.shape[1]
    d = w_gu.shape[2]
    sub = d // 2 // lanes
    f = w_down.shape[2]
    tf = min(tf, f)
    nj = f // tf
    n_steps = (nj + 1) // 2
    n_tiles = tile_expert.shape[0]
    assert tm % MOE_SUB_ROWS == 0 and xs.shape[0] == n_tiles * tm * sub

    def chunk(k, i, j, na):
        jl = jnp.where(i < na[0], j, n_steps - 1)
        c = 2 * jl + k
        return jnp.where(c < nj, c, max(nj - 2, 0)) if k else c

    def ii(i, na):
        return jnp.minimum(i, na[0] - 1)

    def w_specs(k):
        return [
            pl.BlockSpec((None, None, d, tf), lambda i, j, te, tr, na: (layer, te[i], 0, chunk(k, i, j, na))),
            pl.BlockSpec((None, None, d, tf), lambda i, j, te, tr, na: (layer, te[i], 0, chunk(k, i, j, na) + nj)),
            pl.BlockSpec((None, None, tf, d), lambda i, j, te, tr, na: (layer, te[i], chunk(k, i, j, na), 0)),
        ]

    grid_spec = pltpu.PrefetchScalarGridSpec(
        num_scalar_prefetch=3,
        grid=(n_tiles, n_steps),
        in_specs=[pl.BlockSpec((tm * sub, lanes), lambda i, j, te, tr, na: (ii(i, na), 0))] + w_specs(0) + w_specs(1),
        out_specs=pl.BlockSpec((tm * sub, lanes), lambda i, j, te, tr, na: (i, 0)),
        scratch_shapes=[
            pltpu.VMEM((tm, d), BF16),
            pltpu.VMEM((tm, d), F32),
        ],
    )
    return pl.pallas_call(
        functools.partial(_moe_kernel, tm=tm, nj=nj),
        grid_spec=grid_spec,
        out_shape=jax.ShapeDtypeStruct((n_tiles * tm * sub, lanes), jnp.uint32),
        compiler_params=_params(("arbitrary", "arbitrary"), 60),
        name="moe_experts",
    )(tile_expert, tile_rows, n_active, xs, w_gu, w_gu, w_down, w_gu, w_gu, w_down)


def _combine_kernel(p0_ref, p1_ref, ys_hbm, h_ref, wt_ref, gf_ref, o_ref, buf, sem, *, tc, final_norm):
    i = pl.program_id(0)
    n = pl.num_programs(0)
    half = o_ref.shape[1] // 2
    sub = half // LANES

    def start_gather(tile, slot):
        base = tile * tc

        def body(r, carry):
            dst = pl.ds(pl.multiple_of(r * sub, sub), sub)
            for kk, p_ref in enumerate((p0_ref, p1_ref)):
                src = pl.ds(pl.multiple_of(p_ref[base + r] * sub, sub), sub)
                pltpu.make_async_copy(ys_hbm.at[src], buf.at[slot, kk, dst], sem.at[slot]).start()
            return carry

        lax.fori_loop(0, tc, body, 0, unroll=GATHER_UNROLL)

    @pl.when(i == 0)
    def _():
        start_gather(0, 0)

    slot = i % 2
    for kk in range(TOP_K):
        pltpu.make_async_copy(ys_hbm.at[pl.ds(0, tc * sub)], buf.at[slot, kk], sem.at[slot]).wait()

    @pl.when(i + 1 < n)
    def _():
        start_gather(i + 1, (i + 1) % 2)

    wt = wt_ref[...].T
    w0, w1 = wt[:, 0:1], wt[:, 1:2]
    for sl in range(sub):
        a_hi, a_lo = _unpack_pairs_f32(_load_tile_rows(buf.at[slot, 0], sl, tc, sub))
        b_hi, b_lo = _unpack_pairs_f32(_load_tile_rows(buf.at[slot, 1], sl, tc, sub))
        c_hi = slice(sl * LANES, (sl + 1) * LANES)
        c_lo = slice(half + sl * LANES, half + (sl + 1) * LANES)
        o_ref[:, c_hi] = h_ref[:, c_hi] + (w0 * a_hi + w1 * b_hi)
        o_ref[:, c_lo] = h_ref[:, c_lo] + (w0 * a_lo + w1 * b_lo)
    if final_norm:
        o_ref[...] = _rms(o_ref[...], gf_ref[...])


def moe_combine(ys, pos, wts, h, final_gain, *, tc=256):
    t, d = h.shape
    lanes = ys.shape[1]
    sub = d // 2 // lanes
    tc = min(tc, t)
    final_norm = final_gain is not None
    gf = (final_gain if final_norm else jnp.ones((d,), F32)).reshape(1, d)
    grid_spec = pltpu.PrefetchScalarGridSpec(
        num_scalar_prefetch=2,
        grid=(t // tc,),
        in_specs=[
            pl.BlockSpec(memory_space=pl.ANY),
            pl.BlockSpec((tc, d), lambda i, p0, p1: (i, 0)),
            pl.BlockSpec((N_EXPERTS, tc), lambda i, p0, p1: (0, i)),
            pl.BlockSpec((1, d), lambda i, p0, p1: (0, 0)),
        ],
        out_specs=pl.BlockSpec((tc, d), lambda i, p0, p1: (i, 0)),
        scratch_shapes=[
            pltpu.VMEM((2, TOP_K, tc * sub, lanes), jnp.uint32),
            pltpu.SemaphoreType.DMA((2,)),
        ],
    )
    return pl.pallas_call(
        functools.partial(_combine_kernel, tc=tc, final_norm=final_norm),
        grid_spec=grid_spec,
        out_shape=jax.ShapeDtypeStruct((t, d), F32),
        compiler_params=_params(("arbitrary",), 48),
        name="moe_combine",
    )(pos[0], pos[1], ys, h, wts, gf)


def moe_block(h, g, w_router, w_gu, w_down, layer, final_gain, xs_prev, *, tm=1024):
    t = h.shape[0]
    tm = min(tm, t)
    xn, idx, wts = router(h, g, w_router)
    pos, tile_expert, tile_rows, n_active = _routing_tables(idx[:TOP_K], tm)
    xs = moe_dispatch(xn, pos, tile_expert.shape[0] * tm, xs_prev)
    ys = moe_experts(xs, tile_expert, tile_rows, n_active, w_gu, w_down, layer, tm=tm)
    return moe_combine(ys, pos, wts, h, final_gain), xs


def kernel(x, norm_mix, norm_ffn, norm_final, ev_w_in, ev_pool_w, ev_pool_scale, ev_w_out,
           od_w_in, od_conv_w, od_w_out, ffn_w_gu, ffn_w_down, moe_router, moe_w_gu, moe_w_down):
    batch, seq, d = x.shape
    depth = norm_mix.shape[0]
    h = x.reshape(batch * seq, d)
    ev_w_in16, od_w_in16 = ev_w_in.astype(BF16), od_w_in.astype(BF16)
    ev_w_out16, od_w_out16 = ev_w_out.astype(BF16), od_w_out.astype(BF16)
    ffn_w_gu16, ffn_w_down16 = ffn_w_gu.astype(BF16), ffn_w_down.astype(BF16)
    xs = None
    for layer in range(depth):
        i = layer // 2
        if layer % 2 == 0:
            proj = norm_matmul(h, norm_mix[layer], ev_w_in16, i)
            y = even_core(proj, ev_pool_w[i].astype(BF16), ev_pool_scale[i], batch, seq)
            h = matmul_residual(y, ev_w_out16, i, h)
            h = ffn(h, norm_ffn[layer], ffn_w_gu16, ffn_w_down16, i)
        else:
            proj = norm_matmul(h, norm_mix[layer], od_w_in16, i)
            h = conv_out(proj, od_conv_w[i], od_w_out16, i, h, batch, seq)
            final_gain = norm_final if layer == depth - 1 else None
            h, xs = moe_block(h, norm_ffn[layer], moe_router[i], moe_w_gu, moe_w_down, i, final_gain, xs)
    return h.reshape(batch, seq, d)
```

```python
import functools
import math

import jax
import jax.numpy as jnp
from jax import lax
from jax.experimental import pallas as pl
from jax.experimental.pallas import tpu as pltpu

EPS = 1e-6
CHUNK = 64
POOL_WINDOWS = (2, 4, 8, 16)
POOL_GROUP_DIM = 256
RET_HEADS = 4
RET_HEAD_DIM = 256
ROPE_BASE = 10000.0
N_EXPERTS = 8
TOP_K = 2

RET_BLOCK = 256
POOL_HALO = 128
CONV_HALO = 16
ROUTER_LANES = 128
GATHER_UNROLL = 8
LANES = 128
MOE_SUB_ROWS = 256

BF16 = jnp.bfloat16
F32 = jnp.float32
MIB = 1024 * 1024


def _params(semantics, vmem_mib):
    return pltpu.CompilerParams(dimension_semantics=semantics, vmem_limit_bytes=vmem_mib * MIB)


def _rms(x, g):
    ms = jnp.mean(x * x, axis=-1, keepdims=True)
    return x * lax.rsqrt(ms + EPS) * g


def _silu(x):
    return x / (1.0 + jnp.exp(-x))


def _norm_matmul_kernel(x_ref, g_ref, w_ref, o_ref, xn_ref):
    j = pl.program_id(1)

    @pl.when(j == 0)
    def _():
        xn = _rms(x_ref[...], g_ref[...]).astype(xn_ref.dtype)
        xn_ref[...] = xn
        o_ref[...] = jnp.dot(xn, w_ref[...], preferred_element_type=F32).astype(o_ref.dtype)

    @pl.when(j > 0)
    def _():
        o_ref[...] = jnp.dot(xn_ref[...], w_ref[...], preferred_element_type=F32).astype(o_ref.dtype)


def norm_matmul(x, g, w, layer, *, tm=1024, col_blocks=4):
    t, d = x.shape
    n = w.shape[2]
    tm = min(tm, t)
    tn = n // col_blocks if n % (col_blocks * 2 * LANES) == 0 else n
    return pl.pallas_call(
        _norm_matmul_kernel,
        grid=(t // tm, n // tn),
        in_specs=[
            pl.BlockSpec((tm, d), lambda i, j: (i, 0)),
            pl.BlockSpec((1, d), lambda i, j: (0, 0)),
            pl.BlockSpec((None, d, tn), lambda i, j: (layer, 0, j)),
        ],
        out_specs=pl.BlockSpec((tm, tn), lambda i, j: (i, j)),
        out_shape=jax.ShapeDtypeStruct((t, n), BF16),
        scratch_shapes=[pltpu.VMEM((tm, d), BF16)],
        compiler_params=_params(("arbitrary", "arbitrary"), 48),
        name="norm_matmul",
    )(x, g.reshape(1, d), w)


def _even_core_kernel(u_ref, uh_ref, q_ref, k_ref, v_ref, g_ref, cos_ref, sin_ref, dm_ref,
                      wp_ref, ps_ref, y_ref, st_ref, *, ts, log_g):
    s = pl.program_id(1)
    gd = POOL_GROUP_DIM
    pw = len(POOL_WINDOWS) * gd

    @pl.when(s == 0)
    def _():
        st_ref[...] = jnp.zeros_like(st_ref)

    row = lax.broadcasted_iota(jnp.int32, (ts, ts), 0)
    col = lax.broadcasted_iota(jnp.int32, (ts, ts), 1)
    dist = row - col
    hrow = lax.broadcasted_iota(jnp.int32, (ts, POOL_HALO), 0)
    hcol = lax.broadcasted_iota(jnp.int32, (ts, POOL_HALO), 1)
    hdist = hrow - hcol + POOL_HALO
    t_seq = s * ts + lax.broadcasted_iota(jnp.int32, (ts, 1), 0)
    uh = jnp.where(s > 0, uh_ref[...], jnp.zeros_like(uh_ref))
    for g, w in enumerate(POOL_WINDOWS):
        cs = slice(g * gd, (g + 1) * gd)
        ug = u_ref[:, cs]
        band = jnp.logical_and(dist >= 0, dist < w).astype(BF16)
        hband = (hdist < w).astype(BF16)
        wsum = (jnp.dot(band, ug, preferred_element_type=F32)
                + jnp.dot(hband, uh[:, cs], preferred_element_type=F32))
        count = jnp.minimum(t_seq + 1, w).astype(F32)
        p = (wsum / count - ug.astype(F32)).astype(BF16)
        yg = jnp.dot(p, wp_ref[g], preferred_element_type=F32) * ps_ref[:, cs]
        y_ref[:, cs] = yg.astype(y_ref.dtype)

    lb = RET_BLOCK
    hd2 = RET_HEAD_DIM // 2
    n_idx = lax.broadcasted_iota(jnp.int32, (lb, 1), 0).astype(F32)
    for hd in range(RET_HEADS):
        lg = log_g[hd]
        q_decay = jnp.exp(lg * (n_idx + 1.0))
        k_decay = jnp.exp(lg * (lb - 1.0 - n_idx))
        block_decay = math.exp(lg * lb)
        c0 = hd * RET_HEAD_DIM
        for r in range(ts // lb):
            rows = slice(r * lb, (r + 1) * lb)
            cs_, sn_ = cos_ref[rows, :], sin_ref[rows, :]

            def rope(ref):
                x1 = ref[rows, c0:c0 + hd2].astype(F32)
                x2 = ref[rows, c0 + hd2:c0 + 2 * hd2].astype(F32)
                return jnp.concatenate([x1 * cs_ - x2 * sn_, x2 * cs_ + x1 * sn_], axis=-1)

            q = rope(q_ref) * (RET_HEAD_DIM ** -0.5)
            k = rope(k_ref)
            v = v_ref[rows, c0:c0 + RET_HEAD_DIM]
            scores = lax.dot_general(q.astype(BF16), k.astype(BF16), (((1,), (1,)), ((), ())),
                                     preferred_element_type=F32) * dm_ref[hd]
            state = st_ref[hd]
            o = (jnp.dot(scores.astype(BF16), v, preferred_element_type=F32)
                 + jnp.dot((q * q_decay).astype(BF16), state.astype(BF16), preferred_element_type=F32))
            st_ref[hd] = state * block_decay + lax.dot_general(
                (k * k_decay).astype(BF16), v, (((0,), (0,)), ((), ())), preferred_element_type=F32)
            mu = jnp.mean(o, axis=-1, keepdims=True)
            oc = o - mu
            var = jnp.mean(oc * oc, axis=-1, keepdims=True)
            gate = g_ref[rows, c0:c0 + RET_HEAD_DIM].astype(F32)
            y_ref[rows, pw + c0:pw + c0 + RET_HEAD_DIM] = (
                oc * lax.rsqrt(var + 1e-5) * _silu(gate)).astype(y_ref.dtype)


def even_core(proj, w_pool, pool_scale, batch, seq, *, ts=512):
    t = proj.shape[0]
    pw = len(POOL_WINDOWS) * POOL_GROUP_DIM
    rw = RET_HEADS * RET_HEAD_DIM
    assert pw == rw and proj.shape[1] == pw + 4 * rw
    ts = min(ts, seq)
    assert ts % RET_BLOCK == 0 and RET_BLOCK % CHUNK == 0 and ts % POOL_HALO == 0
    nsb = seq // ts
    half = RET_HEAD_DIM // 2

    pos = jnp.arange(seq, dtype=F32)
    inv = ROPE_BASE ** (-jnp.arange(half, dtype=F32) / half)
    ang = pos[:, None] * inv[None, :]
    cos, sin = jnp.cos(ang), jnp.sin(ang)
    log_g = tuple(math.log(1.0 - 2.0 ** (-5.0 - h)) for h in range(RET_HEADS))
    idx = jnp.arange(RET_BLOCK)
    visible = (idx[None, :] // CHUNK) <= (idx[:, None] // CHUNK)
    gap = jnp.abs(idx[:, None] - idx[None, :]).astype(F32)
    dmask = jnp.stack([jnp.where(visible, jnp.exp(lg * gap), 0.0) for lg in log_g]).astype(F32)

    def col(c):
        return pl.BlockSpec((ts, pw), lambda b, s: (b * nsb + s, c))

    halo = pl.BlockSpec(
        (POOL_HALO, pw), lambda b, s: (jnp.maximum((b * nsb + s) * (ts // POOL_HALO) - 1, 0), 0))
    return pl.pallas_call(
        functools.partial(_even_core_kernel, ts=ts, log_g=log_g),
        grid=(batch, nsb),
        in_specs=[
            col(0), halo, col(1), col(2), col(3), col(4),
            pl.BlockSpec((ts, half), lambda b, s: (s, 0)),
            pl.BlockSpec((ts, half), lambda b, s: (s, 0)),
            pl.BlockSpec((RET_HEADS, RET_BLOCK, RET_BLOCK), lambda b, s: (0, 0, 0)),
            pl.BlockSpec(w_pool.shape, lambda b, s: (0, 0, 0)),
            pl.BlockSpec((1, pw), lambda b, s: (0, 0)),
        ],
        out_specs=pl.BlockSpec((ts, pw + rw), lambda b, s: (b * nsb + s, 0)),
        out_shape=jax.ShapeDtypeStruct((t, pw + rw), BF16),
        scratch_shapes=[pltpu.VMEM((RET_HEADS, RET_HEAD_DIM, RET_HEAD_DIM), F32)],
        compiler_params=_params(("arbitrary", "arbitrary"), 48),
        name="even_core",
    )(proj, proj, proj, proj, proj, proj, cos, sin, dmask, w_pool, pool_scale.reshape(1, pw))


def _matmul_residual_kernel(y_ref, w_ref, h_ref, o_ref):
    o_ref[...] = h_ref[...] + jnp.dot(y_ref[...], w_ref[...], preferred_element_type=F32)


def matmul_residual(y, w, layer, h, *, tm=512):
    t, k = y.shape
    d = w.shape[2]
    tm = min(tm, t)
    return pl.pallas_call(
        _matmul_residual_kernel,
        grid=(t // tm,),
        in_specs=[
            pl.BlockSpec((tm, k), lambda i: (i, 0)),
            pl.BlockSpec((None, k, d), lambda i: (layer, 0, 0)),
            pl.BlockSpec((tm, d), lambda i: (i, 0)),
        ],
        out_specs=pl.BlockSpec((tm, d), lambda i: (i, 0)),
        out_shape=jax.ShapeDtypeStruct((t, d), F32),
        compiler_params=_params(("arbitrary",), 48),
        name="matmul_residual",
    )(y, w, h)


def _ffn_kernel(x_ref, g_ref, wg_ref, wu_ref, wd_ref, o_ref, xn_ref):
    j = pl.program_id(1)

    def swiglu(xn):
        gate = jnp.dot(xn, wg_ref[...], preferred_element_type=F32)
        up = jnp.dot(xn, wu_ref[...], preferred_element_type=F32)
        act = (_silu(gate) * up).astype(BF16)
        return jnp.dot(act, wd_ref[...], preferred_element_type=F32)

    @pl.when(j == 0)
    def _():
        x = x_ref[...]
        xn = _rms(x, g_ref[...]).astype(xn_ref.dtype)
        xn_ref[...] = xn
        o_ref[...] = x + swiglu(xn)

    @pl.when(j > 0)
    def _():
        o_ref[...] += swiglu(xn_ref[...])


def ffn(h, g, w_gu, w_down, layer, *, tm=1024, tf=512):
    t, d = h.shape
    f = w_down.shape[1]
    tm, tf = min(tm, t), min(tf, f)
    nj = f // tf
    return pl.pallas_call(
        _ffn_kernel,
        grid=(t // tm, nj),
        in_specs=[
            pl.BlockSpec((tm, d), lambda i, j: (i, 0)),
            pl.BlockSpec((1, d), lambda i, j: (0, 0)),
            pl.BlockSpec((None, d, tf), lambda i, j: (layer, 0, j)),
            pl.BlockSpec((None, d, tf), lambda i, j: (layer, 0, j + nj)),
            pl.BlockSpec((None, tf, d), lambda i, j: (layer, j, 0)),
        ],
        out_specs=pl.BlockSpec((tm, d), lambda i, j: (i, 0)),
        out_shape=jax.ShapeDtypeStruct((t, d), F32),
        scratch_shapes=[pltpu.VMEM((tm, d), BF16)],
        compiler_params=_params(("arbitrary", "arbitrary"), 56),
        name="ffn",
    )(h, g.reshape(1, d), w_gu, w_gu, w_down)


def _conv_out_kernel(bg_ref, cg_ref, hx_ref, cgh_ref, hxh_ref, cw_ref, w_ref, h_ref, o_ref, *, ts):
    s = pl.program_id(1)
    z = cg_ref[...].astype(F32) * hx_ref[...].astype(F32)
    zh = cgh_ref[...].astype(F32) * hxh_ref[...].astype(F32)
    zh = jnp.where(s > 0, zh, 0.0)
    prev1 = zh[CONV_HALO - 1:CONV_HALO, :]
    prev2 = zh[CONV_HALO - 2:CONV_HALO - 1, :]
    row = lax.broadcasted_iota(jnp.int32, (ts, 1), 0)
    z1 = jnp.where(row == 0, prev1, pltpu.roll(z, 1, 0))
    z2 = jnp.where(row == 0, prev2, jnp.where(row == 1, prev1, pltpu.roll(z, 2, 0)))
    cw = cw_ref[...]
    conv = z2 * cw[0:1, :] + z1 * cw[1:2, :] + z * cw[2:3, :]
    act = (bg_ref[...].astype(F32) * conv).astype(BF16)
    o_ref[...] = h_ref[...] + jnp.dot(act, w_ref[...], preferred_element_type=F32)


def conv_out(proj, conv_w, w_out, layer, h, batch, seq, *, ts=512):
    t, d = h.shape
    assert proj.shape[1] == 3 * d and conv_w.shape[0] == 3
    ts = min(ts, seq)
    nsb = seq // ts

    def col(c):
        return pl.BlockSpec((ts, d), lambda b, s: (b * nsb + s, c))

    def halo(c):
        return pl.BlockSpec(
            (CONV_HALO, d), lambda b, s: (jnp.maximum((b * nsb + s) * (ts // CONV_HALO) - 1, 0), c))

    return pl.pallas_call(
        functools.partial(_conv_out_kernel, ts=ts),
        grid=(batch, nsb),
        in_specs=[
            col(0), col(1), col(2), halo(1), halo(2),
            pl.BlockSpec(conv_w.shape, lambda b, s: (0, 0)),
            pl.BlockSpec((None, d, d), lambda b, s: (layer, 0, 0)),
            pl.BlockSpec((ts, d), lambda b, s: (b * nsb + s, 0)),
        ],
        out_specs=pl.BlockSpec((ts, d), lambda b, s: (b * nsb + s, 0)),
        out_shape=jax.ShapeDtypeStruct((t, d), F32),
        compiler_params=_params(("arbitrary", "arbitrary"), 56),
        name="conv_out",
    )(proj, proj, proj, proj, proj, conv_w, w_out, h)


def _pack_bf16_pairs(x):
    n = x.shape[1] // 2
    hi = lax.bitcast_convert_type(x[:, :n].astype(BF16).astype(F32), jnp.uint32)
    lo = lax.bitcast_convert_type(x[:, n:].astype(BF16).astype(F32), jnp.uint32)
    return hi | (lo >> 16)


def _unpack_pairs_f32(w):
    hi = lax.bitcast_convert_type(w & jnp.uint32(0xFFFF0000), F32)
    lo = lax.bitcast_convert_type(w << 16, F32)
    return hi, lo


def _store_rows_as_tiles(ref, packed):
    rows = packed.shape[0]
    sub = packed.shape[1] // LANES
    for sl in range(sub):
        ref[pl.ds(sl, rows, stride=sub), :] = packed[:, sl * LANES:(sl + 1) * LANES]


def _load_tile_rows(ref, sl, rows, sub):
    return ref[pl.ds(sl, rows, stride=sub), :]


def _router_kernel(x_ref, g_ref, r_ref, xn_ref, idx_ref, wt_ref):
    xn = _rms(x_ref[...], g_ref[...])
    _store_rows_as_tiles(xn_ref, _pack_bf16_pairs(xn))
    xh = xn.astype(BF16)
    xl = (xn - xh.astype(F32)).astype(BF16)
    ph = jnp.dot(xh, r_ref[...], preferred_element_type=F32)
    pl_ = jnp.dot(xl, r_ref[...], preferred_element_type=F32)
    logits = (ph[:, :ROUTER_LANES] + ph[:, ROUTER_LANES:]) + (pl_[:, :ROUTER_LANES] + pl_[:, ROUTER_LANES:])
    lt = logits.T[:N_EXPERTS, :]
    e_id = lax.broadcasted_iota(jnp.int32, lt.shape, 0)
    m1 = jnp.max(lt, axis=0, keepdims=True)
    i1 = jnp.min(jnp.where(lt == m1, e_id, N_EXPERTS), axis=0, keepdims=True)
    rest = jnp.where(e_id == i1, -jnp.inf, lt)
    m2 = jnp.max(rest, axis=0, keepdims=True)
    i2 = jnp.min(jnp.where(rest == m2, e_id, N_EXPERTS), axis=0, keepdims=True)
    ex = jnp.exp(m2 - m1)
    w1 = 1.0 / (1.0 + ex)
    w2 = ex / (1.0 + ex)
    idx_ref[...] = jnp.where(e_id == 0, i1, jnp.where(e_id == 1, i2, 0))
    wt_ref[...] = jnp.where(e_id == 0, w1, jnp.where(e_id == 1, w2, 0.0))


def router(h, g, w_router, *, tm=1024):
    t, d = h.shape
    e = w_router.shape[1]
    assert e == N_EXPERTS
    tm = min(tm, t)
    sub = d // 2 // LANES
    r_pad = jnp.zeros((d, ROUTER_LANES), F32).at[:, :e].set(w_router)
    r_hi = r_pad.astype(BF16)
    r_lo = (r_pad - r_hi.astype(F32)).astype(BF16)
    r_split = jnp.concatenate([r_hi, r_lo], axis=1)
    return pl.pallas_call(
        _router_kernel,
        grid=(t // tm,),
        in_specs=[
            pl.BlockSpec((tm, d), lambda i: (i, 0)),
            pl.BlockSpec((1, d), lambda i: (0, 0)),
            pl.BlockSpec((d, 2 * ROUTER_LANES), lambda i: (0, 0)),
        ],
        out_specs=[
            pl.BlockSpec((tm * sub, LANES), lambda i: (i, 0)),
            pl.BlockSpec((e, tm), lambda i: (0, i)),
            pl.BlockSpec((e, tm), lambda i: (0, i)),
        ],
        out_shape=[
            jax.ShapeDtypeStruct((t * sub, LANES), jnp.uint32),
            jax.ShapeDtypeStruct((e, t), jnp.int32),
            jax.ShapeDtypeStruct((e, t), F32),
        ],
        compiler_params=_params(("arbitrary",), 48),
        name="router",
    )(h, g.reshape(1, d), r_split)


def _routing_tables(idx, tm):
    k, t = idx.shape
    n_tiles = (k * t) // tm + N_EXPERTS
    flat = idx.reshape(-1)
    onehot = (flat[:, None] == jnp.arange(N_EXPERTS)[None, :]).astype(jnp.int32)
    csum = jnp.cumsum(onehot, axis=0)
    rank = jnp.sum(onehot * (csum - 1), axis=1)
    counts = csum[-1]
    padded = ((counts + tm - 1) // tm) * tm
    g_end = jnp.cumsum(padded)
    g_start = g_end - padded
    pos = jnp.sum(onehot * g_start[None, :], axis=1) + rank
    n_active = (g_end[-1] // tm).astype(jnp.int32)
    tile_start = jnp.arange(n_tiles, dtype=jnp.int32) * tm
    tile_expert = jnp.sum((tile_start[:, None] >= g_end[None, :]).astype(jnp.int32), axis=1)
    tile_expert = jnp.minimum(tile_expert, N_EXPERTS - 1)
    tile_rows = jnp.clip((g_start + counts)[tile_expert] - tile_start, 0, tm)
    is_active = jnp.arange(n_tiles) < n_active
    tile_rows = jnp.where(is_active, tile_rows, 0).astype(jnp.int32)
    last = tile_expert[n_active - 1]
    tile_expert = jnp.where(is_active, tile_expert, last).astype(jnp.int32)
    return pos.reshape(k, t).astype(jnp.int32), tile_expert, tile_rows, n_active.reshape(1)


def _dispatch_kernel(p0_ref, p1_ref, x_ref, init_hbm, o_hbm, sem, *, tt, sub):
    del init_hbm
    base = pl.program_id(0) * tt

    def body(r, carry):
        src = x_ref.at[pl.ds(pl.multiple_of(r * sub, sub), sub)]
        for p_ref in (p0_ref, p1_ref):
            dst = pl.ds(pl.multiple_of(p_ref[base + r] * sub, sub), sub)
            pltpu.make_async_copy(src, o_hbm.at[dst], sem).start()
        return carry

    lax.fori_loop(0, tt, body, 0, unroll=GATHER_UNROLL)
    for _ in range(TOP_K):
        pltpu.make_async_copy(x_ref, o_hbm.at[pl.ds(0, tt * sub)], sem).wait()


def moe_dispatch(xn, pos, n_rows, init, *, tt=1024):
    lanes = xn.shape[1]
    t = pos.shape[1]
    sub = xn.shape[0] // t
    tt = min(tt, t)
    grid_spec = pltpu.PrefetchScalarGridSpec(
        num_scalar_prefetch=2,
        grid=(t // tt,),
        in_specs=[
            pl.BlockSpec((tt * sub, lanes), lambda i, p0, p1: (i, 0)),
            pl.BlockSpec(memory_space=pl.ANY),
        ],
        out_specs=pl.BlockSpec(memory_space=pl.ANY),
        scratch_shapes=[pltpu.SemaphoreType.DMA(())],
    )
    return pl.pallas_call(
        functools.partial(_dispatch_kernel, tt=tt, sub=sub),
        grid_spec=grid_spec,
        out_shape=jax.ShapeDtypeStruct((n_rows * sub, lanes), jnp.uint32),
        input_output_aliases={3: 0},
        compiler_params=_params(("arbitrary",), 32),
        name="moe_dispatch",
    )(pos[0], pos[1], xn, jnp.zeros((n_rows * sub, lanes), jnp.uint32) if init is None else init)


def _moe_kernel(te_ref, tr_ref, na_ref, x_ref, wg0_ref, wu0_ref, wd0_ref, wg1_ref, wu1_ref, wd1_ref, o_ref,
                xb, acc, *, tm, nj):
    j = pl.program_id(1)
    n_rows = tr_ref[pl.program_id(0)]
    half = xb.shape[1] // 2
    sub = half // LANES
    n_steps = (nj + 1) // 2
    has_second = 2 * j + 1 < nj
    whole = n_rows > tm - MOE_SUB_ROWS

    def unpack_rows():
        pieces = [_unpack_pairs_f32(_load_tile_rows(x_ref, sl, tm, sub)) for sl in range(sub)]
        return jnp.concatenate([p[0].astype(BF16) for p in pieces] + [p[1].astype(BF16) for p in pieces], axis=1)

    @pl.when(jnp.logical_and(j == 0, jnp.logical_not(whole)))
    def _():
        @pl.when(n_rows > 0)
        def _():
            xb[...] = unpack_rows()

        acc[...] = jnp.zeros_like(acc)

    def weights(k):
        wg_ref, wu_ref, wd_ref = ((wg0_ref, wu0_ref, wd0_ref), (wg1_ref, wu1_ref, wd1_ref))[k]
        return wg_ref[...].astype(BF16), wu_ref[...].astype(BF16), wd_ref[...].astype(BF16)

    def swiglu_rows(rows, w, assign=False, x=None, final=False):
        wg, wu, wd = w
        x = xb[rows, :] if x is None else x
        gate = jnp.dot(x, wg, preferred_element_type=F32)
        up = jnp.dot(x, wu, preferred_element_type=F32)
        act = (_silu(gate) * up).astype(BF16)
        y = jnp.dot(act, wd, preferred_element_type=F32)
        if final:
            _store_rows_as_tiles(o_ref, _pack_bf16_pairs(y if assign else acc[rows, :] + y))
        elif assign:
            acc[rows, :] = y
        else:
            acc[rows, :] += y

    def whole_step(n_chunks, first, last):
        x = None
        if first:
            x = unpack_rows()
            xb[...] = x
        w = weights(0)
        if n_chunks == 2:
            (g0, u0, d0), (g1, u1, d1) = w, weights(1)
            w = (jnp.concatenate([g0, g1], axis=1), jnp.concatenate([u0, u1], axis=1),
                 jnp.concatenate([d0, d1], axis=0))
        swiglu_rows(slice(None), w, assign=first, x=x, final=last)

    n_last = 1 if nj % 2 else 2
    if n_steps == 1:
        pl.when(whole)(functools.partial(whole_step, n_last, True, True))
    else:
        pl.when(jnp.logical_and(whole, j == 0))(functools.partial(whole_step, 2, True, False))
        if n_steps > 2:
            middle = jnp.logical_and(j > 0, j < n_steps - 1)
            pl.when(jnp.logical_and(whole, middle))(functools.partial(whole_step, 2, False, False))
        pl.when(jnp.logical_and(whole, j == n_steps - 1))(functools.partial(whole_step, n_last, False, True))

    @pl.when(jnp.logical_and(n_rows > 0, jnp.logical_not(whole)))
    def _():
        for k in range(2):
            @pl.when(jnp.logical_or(k == 0, has_second))
            def _():
                w = weights(k)
                for sb in range(tm // MOE_SUB_ROWS):
                    @pl.when(sb * MOE_SUB_ROWS < n_rows)
                    def _():
                        swiglu_rows(slice(sb * MOE_SUB_ROWS, (sb + 1) * MOE_SUB_ROWS), w)

    @pl.when(jnp.logical_and(j == n_steps - 1, jnp.logical_not(whole)))
    def _():
        _store_rows_as_tiles(o_ref, _pack_bf16_pairs(acc[...]))


def moe_experts(xs, tile_expert, tile_rows, n_active, w_gu, w_down, layer, *, tm, tf=256):
    lanes = xs.shape[1]
    d = w_gu.shape[2]
    sub = d // 2 // lanes
    f = w_down.shape[2]
    tf = min(tf, f)
    nj = f // tf
    n_steps = (nj + 1) // 2
    n_tiles = tile_expert.shape[0]
    assert tm % MOE_SUB_ROWS == 0 and xs.shape[0] == n_tiles * tm * sub

    def chunk(k, i, j, na):
        jl = jnp.where(i < na[0], j, n_steps - 1)
        c = 2 * jl + k
        return jnp.where(c < nj, c, max(nj - 2, 0)) if k else c

    def ii(i, na):
        return jnp.minimum(i, na[0] - 1)

    def w_specs(k):
        return [
            pl.BlockSpec((None, None, d, tf), lambda i, j, te, tr, na: (layer, te[i], 0, chunk(k, i, j, na))),
            pl.BlockSpec((None, None, d, tf), lambda i, j, te, tr, na: (layer, te[i], 0, chunk(k, i, j, na) + nj)),
            pl.BlockSpec((None, None, tf, d), lambda i, j, te, tr, na: (layer, te[i], chunk(k, i, j, na), 0)),
        ]

    grid_spec = pltpu.PrefetchScalarGridSpec(
        num_scalar_prefetch=3,
        grid=(n_tiles, n_steps),
        in_specs=[pl.BlockSpec((tm * sub, lanes), lambda i, j, te, tr, na: (ii(i, na), 0))] + w_specs(0) + w_specs(1),
        out_specs=pl.BlockSpec((tm * sub, lanes), lambda i, j, te, tr, na: (i, 0)),
        scratch_shapes=[
            pltpu.VMEM((tm, d), BF16),
            pltpu.VMEM((tm, d), F32),
        ],
    )
    return pl.pallas_call(
        functools.partial(_moe_kernel, tm=tm, nj=nj),
        grid_spec=grid_spec,
        out_shape=jax.ShapeDtypeStruct((n_tiles * tm * sub, lanes), jnp.uint32),
        compiler_params=_params(("arbitrary", "arbitrary"), 60),
        name="moe_experts",
    )(tile_expert, tile_rows, n_active, xs, w_gu, w_gu, w_down, w_gu, w_gu, w_down)


def _combine_kernel(p0_ref, p1_ref, ys_hbm, h_ref, wt_ref, gf_ref, o_ref, buf, sem, *, tc, final_norm):
    i = pl.program_id(0)
    n = pl.num_programs(0)
    half = o_ref.shape[1] // 2
    sub = half // LANES

    def start_gather(tile, slot):
        base = tile * tc

        def body(r, carry):
            dst = pl.ds(pl.multiple_of(r * sub, sub), sub)
            for kk, p_ref in enumerate((p0_ref, p1_ref)):
                src = pl.ds(pl.multiple_of(p_ref[base + r] * sub, sub), sub)
                pltpu.make_async_copy(ys_hbm.at[src], buf.at[slot, kk, dst], sem.at[slot]).start()
            return carry

        lax.fori_loop(0, tc, body, 0, unroll=GATHER_UNROLL)

    @pl.when(i == 0)
    def _():
        start_gather(0, 0)

    slot = i % 2
    for kk in range(TOP_K):
        pltpu.make_async_copy(ys_hbm.at[pl.ds(0, tc * sub)], buf.at[slot, kk], sem.at[slot]).wait()

    @pl.when(i + 1 < n)
    def _():
        start_gather(i + 1, (i + 1) % 2)

    wt = wt_ref[...].T
    w0, w1 = wt[:, 0:1], wt[:, 1:2]
    for sl in range(sub):
        a_hi, a_lo = _unpack_pairs_f32(_load_tile_rows(buf.at[slot, 0], sl, tc, sub))
        b_hi, b_lo = _unpack_pairs_f32(_load_tile_rows(buf.at[slot, 1], sl, tc, sub))
        c_hi = slice(sl * LANES, (sl + 1) * LANES)
        c_lo = slice(half + sl * LANES, half + (sl + 1) * LANES)
        o_ref[:, c_hi] = h_ref[:, c_hi] + (w0 * a_hi + w1 * b_hi)
        o_ref[:, c_lo] = h_ref[:, c_lo] + (w0 * a_lo + w1 * b_lo)
    if final_norm:
        o_ref[...] = _rms(o_ref[...], gf_ref[...])


def moe_combine(ys, pos, wts, h, final_gain, *, tc=512):
    t, d = h.shape
    lanes = ys.shape[1]
    sub = d // 2 // lanes
    tc = min(tc, t)
    final_norm = final_gain is not None
    gf = (final_gain if final_norm else jnp.ones((d,), F32)).reshape(1, d)
    grid_spec = pltpu.PrefetchScalarGridSpec(
        num_scalar_prefetch=2,
        grid=(t // tc,),
        in_specs=[
            pl.BlockSpec(memory_space=pl.ANY),
            pl.BlockSpec((tc, d), lambda i, p0, p1: (i, 0)),
            pl.BlockSpec((N_EXPERTS, tc), lambda i, p0, p1: (0, i)),
            pl.BlockSpec((1, d), lambda i, p0, p1: (0, 0)),
        ],
        out_specs=pl.BlockSpec((tc, d), lambda i, p0, p1: (i, 0)),
        scratch_shapes=[
            pltpu.VMEM((2, TOP_K, tc * sub, lanes), jnp.uint32),
            pltpu.SemaphoreType.DMA((2,)),
        ],
    )
    return pl.pallas_call(
        functools.partial(_combine_kernel, tc=tc, final_norm=final_norm),
        grid_spec=grid_spec,
        out_shape=jax.ShapeDtypeStruct((t, d), F32),
        compiler_params=_params(("arbitrary",), 48),
        name="moe_combine",
    )(pos[0], pos[1], ys, h, wts, gf)


def moe_block(h, g, w_router, w_gu, w_down, layer, final_gain, xs_prev, *, tm=1024):
    t = h.shape[0]
    tm = min(tm, t)
    xn, idx, wts = router(h, g, w_router)
    pos, tile_expert, tile_rows, n_active = _routing_tables(idx[:TOP_K], tm)
    xs = moe_dispatch(xn, pos, tile_expert.shape[0] * tm, xs_prev)
    ys = moe_experts(xs, tile_expert, tile_rows, n_active, w_gu, w_down, layer, tm=tm)
    return moe_combine(ys, pos, wts, h, final_gain), xs


def kernel(x, norm_mix, norm_ffn, norm_final, ev_w_in, ev_pool_w, ev_pool_scale, ev_w_out,
           od_w_in, od_conv_w, od_w_out, ffn_w_gu, ffn_w_down, moe_router, moe_w_gu, moe_w_down):
    batch, seq, d = x.shape
    depth = norm_mix.shape[0]
    h = x.reshape(batch * seq, d)
    ev_w_in16, od_w_in16 = ev_w_in.astype(BF16), od_w_in.astype(BF16)
    ev_w_out16, od_w_out16 = ev_w_out.astype(BF16), od_w_out.astype(BF16)
    ffn_w_gu16, ffn_w_down16 = ffn_w_gu.astype(BF16), ffn_w_down.astype(BF16)
    xs = None
    for layer in range(depth):
        i = layer // 2
        if layer % 2 == 0:
            proj = norm_matmul(h, norm_mix[layer], ev_w_in16, i)
            y = even_core(proj, ev_pool_w[i].astype(BF16), ev_pool_scale[i], batch, seq)
            h = matmul_residual(y, ev_w_out16, i, h)
            h = ffn(h, norm_ffn[layer], ffn_w_gu16, ffn_w_down16, i)
        else:
            proj = norm_matmul(h, norm_mix[layer], od_w_in16, i)
            h = conv_out(proj, od_conv_w[i], od_w_out16, i, h, batch, seq)
            final_gain = norm_final if layer == depth - 1 else None
            h, xs = moe_block(h, norm_ffn[layer], moe_router[i], moe_w_gu, moe_w_down, i, final_gain, xs)
    return h.reshape(batch, seq, d)
```

```python
import functools
import math

import jax
import jax.numpy as jnp
from jax import lax
from jax.experimental import pallas as pl
from jax.experimental.pallas import tpu as pltpu

EPS = 1e-6
CHUNK = 64
POOL_WINDOWS = (2, 4, 8, 16)
POOL_GROUP_DIM = 256
RET_HEADS = 4
RET_HEAD_DIM = 256
ROPE_BASE = 10000.0
N_EXPERTS = 8
TOP_K = 2

RET_BLOCK = 256
POOL_HALO = 128
CONV_HALO = 16
ROUTER_LANES = 128
GATHER_UNROLL = 8
LANES = 128
MOE_SUB_ROWS = 256

BF16 = jnp.bfloat16
F32 = jnp.float32
MIB = 1024 * 1024


def _params(semantics, vmem_mib):
    return pltpu.CompilerParams(dimension_semantics=semantics, vmem_limit_bytes=vmem_mib * MIB)


def _rms(x, g):
    ms = jnp.mean(x * x, axis=-1, keepdims=True)
    return x * lax.rsqrt(ms + EPS) * g


def _silu(x):
    return x / (1.0 + jnp.exp(-x))


def _norm_matmul_kernel(x_ref, g_ref, w_ref, o_ref, xn_ref):
    j = pl.program_id(1)

    @pl.when(j == 0)
    def _():
        xn = _rms(x_ref[...], g_ref[...]).astype(xn_ref.dtype)
        xn_ref[...] = xn
        o_ref[...] = jnp.dot(xn, w_ref[...], preferred_element_type=F32).astype(o_ref.dtype)

    @pl.when(j > 0)
    def _():
        o_ref[...] = jnp.dot(xn_ref[...], w_ref[...], preferred_element_type=F32).astype(o_ref.dtype)


def norm_matmul(x, g, w, layer, *, tm=1024, col_blocks=4):
    t, d = x.shape
    n = w.shape[2]
    tm = min(tm, t)
    tn = n // col_blocks if n % (col_blocks * 2 * LANES) == 0 else n
    return pl.pallas_call(
        _norm_matmul_kernel,
        grid=(t // tm, n // tn),
        in_specs=[
            pl.BlockSpec((tm, d), lambda i, j: (i, 0)),
            pl.BlockSpec((1, d), lambda i, j: (0, 0)),
            pl.BlockSpec((None, d, tn), lambda i, j: (layer, 0, j)),
        ],
        out_specs=pl.BlockSpec((tm, tn), lambda i, j: (i, j)),
        out_shape=jax.ShapeDtypeStruct((t, n), BF16),
        scratch_shapes=[pltpu.VMEM((tm, d), BF16)],
        compiler_params=_params(("arbitrary", "arbitrary"), 48),
        name="norm_matmul",
    )(x, g.reshape(1, d), w)


def _even_core_kernel(u_ref, uh_ref, q_ref, k_ref, v_ref, g_ref, cos_ref, sin_ref, dm_ref,
                      wp_ref, ps_ref, y_ref, st_ref, *, ts, log_g):
    s = pl.program_id(1)
    gd = POOL_GROUP_DIM
    pw = len(POOL_WINDOWS) * gd

    @pl.when(s == 0)
    def _():
        st_ref[...] = jnp.zeros_like(st_ref)

    row = lax.broadcasted_iota(jnp.int32, (ts, ts), 0)
    col = lax.broadcasted_iota(jnp.int32, (ts, ts), 1)
    dist = row - col
    hrow = lax.broadcasted_iota(jnp.int32, (ts, POOL_HALO), 0)
    hcol = lax.broadcasted_iota(jnp.int32, (ts, POOL_HALO), 1)
    hdist = hrow - hcol + POOL_HALO
    t_seq = s * ts + lax.broadcasted_iota(jnp.int32, (ts, 1), 0)
    uh = jnp.where(s > 0, uh_ref[...], jnp.zeros_like(uh_ref))
    for g, w in enumerate(POOL_WINDOWS):
        cs = slice(g * gd, (g + 1) * gd)
        ug = u_ref[:, cs]
        band = jnp.logical_and(dist >= 0, dist < w).astype(BF16)
        hband = (hdist < w).astype(BF16)
        wsum = (jnp.dot(band, ug, preferred_element_type=F32)
                + jnp.dot(hband, uh[:, cs], preferred_element_type=F32))
        count = jnp.minimum(t_seq + 1, w).astype(F32)
        p = (wsum / count - ug.astype(F32)).astype(BF16)
        yg = jnp.dot(p, wp_ref[g], preferred_element_type=F32) * ps_ref[:, cs]
        y_ref[:, cs] = yg.astype(y_ref.dtype)

    lb = RET_BLOCK
    hd2 = RET_HEAD_DIM // 2
    n_idx = lax.broadcasted_iota(jnp.int32, (lb, 1), 0).astype(F32)
    for hd in range(RET_HEADS):
        lg = log_g[hd]
        q_decay = jnp.exp(lg * (n_idx + 1.0))
        k_decay = jnp.exp(lg * (lb - 1.0 - n_idx))
        block_decay = math.exp(lg * lb)
        c0 = hd * RET_HEAD_DIM
        for r in range(ts // lb):
            rows = slice(r * lb, (r + 1) * lb)
            cs_, sn_ = cos_ref[rows, :], sin_ref[rows, :]

            def rope(ref):
                x1 = ref[rows, c0:c0 + hd2].astype(F32)
                x2 = ref[rows, c0 + hd2:c0 + 2 * hd2].astype(F32)
                return jnp.concatenate([x1 * cs_ - x2 * sn_, x2 * cs_ + x1 * sn_], axis=-1)

            q = rope(q_ref) * (RET_HEAD_DIM ** -0.5)
            k = rope(k_ref)
            v = v_ref[rows, c0:c0 + RET_HEAD_DIM]
            scores = lax.dot_general(q.astype(BF16), k.astype(BF16), (((1,), (1,)), ((), ())),
                                     preferred_element_type=F32) * dm_ref[hd]
            state = st_ref[hd]
            o = (jnp.dot(scores.astype(BF16), v, preferred_element_type=F32)
                 + jnp.dot((q * q_decay).astype(BF16), state.astype(BF16), preferred_element_type=F32))
            st_ref[hd] = state * block_decay + lax.dot_general(
                (k * k_decay).astype(BF16), v, (((0,), (0,)), ((), ())), preferred_element_type=F32)
            mu = jnp.mean(o, axis=-1, keepdims=True)
            oc = o - mu
            var = jnp.mean(oc * oc, axis=-1, keepdims=True)
            gate = g_ref[rows, c0:c0 + RET_HEAD_DIM].astype(F32)
            y_ref[rows, pw + c0:pw + c0 + RET_HEAD_DIM] = (
                oc * lax.rsqrt(var + 1e-5) * _silu(gate)).astype(y_ref.dtype)


def even_core(proj, w_pool, pool_scale, batch, seq, *, ts=512):
    t = proj.shape[0]
    pw = len(POOL_WINDOWS) * POOL_GROUP_DIM
    rw = RET_HEADS * RET_HEAD_DIM
    assert pw == rw and proj.shape[1] == pw + 4 * rw
    ts = min(ts, seq)
    assert ts % RET_BLOCK == 0 and RET_BLOCK % CHUNK == 0 and ts % POOL_HALO == 0
    nsb = seq // ts
    half = RET_HEAD_DIM // 2

    pos = jnp.arange(seq, dtype=F32)
    inv = ROPE_BASE ** (-jnp.arange(half, dtype=F32) / half)
    ang = pos[:, None] * inv[None, :]
    cos, sin = jnp.cos(ang), jnp.sin(ang)
    log_g = tuple(math.log(1.0 - 2.0 ** (-5.0 - h)) for h in range(RET_HEADS))
    idx = jnp.arange(RET_BLOCK)
    visible = (idx[None, :] // CHUNK) <= (idx[:, None] // CHUNK)
    gap = jnp.abs(idx[:, None] - idx[None, :]).astype(F32)
    dmask = jnp.stack([jnp.where(visible, jnp.exp(lg * gap), 0.0) for lg in log_g]).astype(F32)

    def col(c):
        return pl.BlockSpec((ts, pw), lambda b, s: (b * nsb + s, c))

    halo = pl.BlockSpec(
        (POOL_HALO, pw), lambda b, s: (jnp.maximum((b * nsb + s) * (ts // POOL_HALO) - 1, 0), 0))
    return pl.pallas_call(
        functools.partial(_even_core_kernel, ts=ts, log_g=log_g),
        grid=(batch, nsb),
        in_specs=[
            col(0), halo, col(1), col(2), col(3), col(4),
            pl.BlockSpec((ts, half), lambda b, s: (s, 0)),
            pl.BlockSpec((ts, half), lambda b, s: (s, 0)),
            pl.BlockSpec((RET_HEADS, RET_BLOCK, RET_BLOCK), lambda b, s: (0, 0, 0)),
            pl.BlockSpec(w_pool.shape, lambda b, s: (0, 0, 0)),
            pl.BlockSpec((1, pw), lambda b, s: (0, 0)),
        ],
        out_specs=pl.BlockSpec((ts, pw + rw), lambda b, s: (b * nsb + s, 0)),
        out_shape=jax.ShapeDtypeStruct((t, pw + rw), BF16),
        scratch_shapes=[pltpu.VMEM((RET_HEADS, RET_HEAD_DIM, RET_HEAD_DIM), F32)],
        compiler_params=_params(("arbitrary", "arbitrary"), 48),
        name="even_core",
    )(proj, proj, proj, proj, proj, proj, cos, sin, dmask, w_pool, pool_scale.reshape(1, pw))


def _matmul_residual_kernel(y_ref, w_ref, h_ref, o_ref):
    o_ref[...] = h_ref[...] + jnp.dot(y_ref[...], w_ref[...], preferred_element_type=F32)


def matmul_residual(y, w, layer, h, *, tm=512):
    t, k = y.shape
    d = w.shape[2]
    tm = min(tm, t)
    return pl.pallas_call(
        _matmul_residual_kernel,
        grid=(t // tm,),
        in_specs=[
            pl.BlockSpec((tm, k), lambda i: (i, 0)),
            pl.BlockSpec((None, k, d), lambda i: (layer, 0, 0)),
            pl.BlockSpec((tm, d), lambda i: (i, 0)),
        ],
        out_specs=pl.BlockSpec((tm, d), lambda i: (i, 0)),
        out_shape=jax.ShapeDtypeStruct((t, d), F32),
        compiler_params=_params(("arbitrary",), 48),
        name="matmul_residual",
    )(y, w, h)


def _ffn_kernel(x_ref, g_ref, wg_ref, wu_ref, wd_ref, o_ref, xn_ref):
    j = pl.program_id(1)

    def swiglu(xn):
        gate = jnp.dot(xn, wg_ref[...], preferred_element_type=F32)
        up = jnp.dot(xn, wu_ref[...], preferred_element_type=F32)
        act = (_silu(gate) * up).astype(BF16)
        return jnp.dot(act, wd_ref[...], preferred_element_type=F32)

    @pl.when(j == 0)
    def _():
        x = x_ref[...]
        xn = _rms(x, g_ref[...]).astype(xn_ref.dtype)
        xn_ref[...] = xn
        o_ref[...] = x + swiglu(xn)

    @pl.when(j > 0)
    def _():
        o_ref[...] += swiglu(xn_ref[...])


def ffn(h, g, w_gu, w_down, layer, *, tm=1024, tf=512):
    t, d = h.shape
    f = w_down.shape[1]
    tm, tf = min(tm, t), min(tf, f)
    nj = f // tf
    return pl.pallas_call(
        _ffn_kernel,
        grid=(t // tm, nj),
        in_specs=[
            pl.BlockSpec((tm, d), lambda i, j: (i, 0)),
            pl.BlockSpec((1, d), lambda i, j: (0, 0)),
            pl.BlockSpec((None, d, tf), lambda i, j: (layer, 0, j)),
            pl.BlockSpec((None, d, tf), lambda i, j: (layer, 0, j + nj)),
            pl.BlockSpec((None, tf, d), lambda i, j: (layer, j, 0)),
        ],
        out_specs=pl.BlockSpec((tm, d), lambda i, j: (i, 0)),
        out_shape=jax.ShapeDtypeStruct((t, d), F32),
        scratch_shapes=[pltpu.VMEM((tm, d), BF16)],
        compiler_params=_params(("arbitrary", "arbitrary"), 56),
        name="ffn",
    )(h, g.reshape(1, d), w_gu, w_gu, w_down)


def _conv_out_kernel(bg_ref, cg_ref, hx_ref, cgh_ref, hxh_ref, cw_ref, w_ref, h_ref, o_ref, *, ts):
    s = pl.program_id(1)
    z = cg_ref[...].astype(F32) * hx_ref[...].astype(F32)
    zh = cgh_ref[...].astype(F32) * hxh_ref[...].astype(F32)
    zh = jnp.where(s > 0, zh, 0.0)
    prev1 = zh[CONV_HALO - 1:CONV_HALO, :]
    prev2 = zh[CONV_HALO - 2:CONV_HALO - 1, :]
    row = lax.broadcasted_iota(jnp.int32, (ts, 1), 0)
    z1 = jnp.where(row == 0, prev1, pltpu.roll(z, 1, 0))
    z2 = jnp.where(row == 0, prev2, jnp.where(row == 1, prev1, pltpu.roll(z, 2, 0)))
    cw = cw_ref[...]
    conv = z2 * cw[0:1, :] + z1 * cw[1:2, :] + z * cw[2:3, :]
    act = (bg_ref[...].astype(F32) * conv).astype(BF16)
    o_ref[...] = h_ref[...] + jnp.dot(act, w_ref[...], preferred_element_type=F32)


def conv_out(proj, conv_w, w_out, layer, h, batch, seq, *, ts=512):
    t, d = h.shape
    assert proj.shape[1] == 3 * d and conv_w.shape[0] == 3
    ts = min(ts, seq)
    nsb = seq // ts

    def col(c):
        return pl.BlockSpec((ts, d), lambda b, s: (b * nsb + s, c))

    def halo(c):
        return pl.BlockSpec(
            (CONV_HALO, d), lambda b, s: (jnp.maximum((b * nsb + s) * (ts // CONV_HALO) - 1, 0), c))

    return pl.pallas_call(
        functools.partial(_conv_out_kernel, ts=ts),
        grid=(batch, nsb),
        in_specs=[
            col(0), col(1), col(2), halo(1), halo(2),
            pl.BlockSpec(conv_w.shape, lambda b, s: (0, 0)),
            pl.BlockSpec((None, d, d), lambda b, s: (layer, 0, 0)),
            pl.BlockSpec((ts, d), lambda b, s: (b * nsb + s, 0)),
        ],
        out_specs=pl.BlockSpec((ts, d), lambda b, s: (b * nsb + s, 0)),
        out_shape=jax.ShapeDtypeStruct((t, d), F32),
        compiler_params=_params(("arbitrary", "arbitrary"), 56),
        name="conv_out",
    )(proj, proj, proj, proj, proj, conv_w, w_out, h)


def _pack_bf16_pairs(x):
    n = x.shape[1] // 2
    hi = lax.bitcast_convert_type(x[:, :n].astype(BF16).astype(F32), jnp.uint32)
    lo = lax.bitcast_convert_type(x[:, n:].astype(BF16).astype(F32), jnp.uint32)
    return hi | (lo >> 16)


def _unpack_pairs_f32(w):
    hi = lax.bitcast_convert_type(w & jnp.uint32(0xFFFF0000), F32)
    lo = lax.bitcast_convert_type(w << 16, F32)
    return hi, lo


def _store_rows_as_tiles(ref, packed):
    rows = packed.shape[0]
    sub = packed.shape[1] // LANES
    for sl in range(sub):
        ref[pl.ds(sl, rows, stride=sub), :] = packed[:, sl * LANES:(sl + 1) * LANES]


def _load_tile_rows(ref, sl, rows, sub):
    return ref[pl.ds(sl, rows, stride=sub), :]


def _router_kernel(x_ref, g_ref, r_ref, xn_ref, idx_ref, wt_ref):
    xn = _rms(x_ref[...], g_ref[...])
    _store_rows_as_tiles(xn_ref, _pack_bf16_pairs(xn))
    xh = xn.astype(BF16)
    xl = (xn - xh.astype(F32)).astype(BF16)
    ph = jnp.dot(xh, r_ref[...], preferred_element_type=F32)
    pl_ = jnp.dot(xl, r_ref[...], preferred_element_type=F32)
    logits = (ph[:, :ROUTER_LANES] + ph[:, ROUTER_LANES:]) + (pl_[:, :ROUTER_LANES] + pl_[:, ROUTER_LANES:])
    lt = logits.T[:N_EXPERTS, :]
    e_id = lax.broadcasted_iota(jnp.int32, lt.shape, 0)
    m1 = jnp.max(lt, axis=0, keepdims=True)
    i1 = jnp.min(jnp.where(lt == m1, e_id, N_EXPERTS), axis=0, keepdims=True)
    rest = jnp.where(e_id == i1, -jnp.inf, lt)
    m2 = jnp.max(rest, axis=0, keepdims=True)
    i2 = jnp.min(jnp.where(rest == m2, e_id, N_EXPERTS), axis=0, keepdims=True)
    ex = jnp.exp(m2 - m1)
    w1 = 1.0 / (1.0 + ex)
    w2 = ex / (1.0 + ex)
    idx_ref[...] = jnp.where(e_id == 0, i1, jnp.where(e_id == 1, i2, 0))
    wt_ref[...] = jnp.where(e_id == 0, w1, jnp.where(e_id == 1, w2, 0.0))


def router(h, g, w_router, *, tm=1024):
    t, d = h.shape
    e = w_router.shape[1]
    assert e == N_EXPERTS
    tm = min(tm, t)
    sub = d // 2 // LANES
    r_pad = jnp.zeros((d, ROUTER_LANES), F32).at[:, :e].set(w_router)
    r_hi = r_pad.astype(BF16)
    r_lo = (r_pad - r_hi.astype(F32)).astype(BF16)
    r_split = jnp.concatenate([r_hi, r_lo], axis=1)
    return pl.pallas_call(
        _router_kernel,
        grid=(t // tm,),
        in_specs=[
            pl.BlockSpec((tm, d), lambda i: (i, 0)),
            pl.BlockSpec((1, d), lambda i: (0, 0)),
            pl.BlockSpec((d, 2 * ROUTER_LANES), lambda i: (0, 0)),
        ],
        out_specs=[
            pl.BlockSpec((tm * sub, LANES), lambda i: (i, 0)),
            pl.BlockSpec((e, tm), lambda i: (0, i)),
            pl.BlockSpec((e, tm), lambda i: (0, i)),
        ],
        out_shape=[
            jax.ShapeDtypeStruct((t * sub, LANES), jnp.uint32),
            jax.ShapeDtypeStruct((e, t), jnp.int32),
            jax.ShapeDtypeStruct((e, t), F32),
        ],
        compiler_params=_params(("arbitrary",), 48),
        name="router",
    )(h, g.reshape(1, d), r_split)


def _routing_tables(idx, tm):
    k, t = idx.shape
    n_tiles = (k * t) // tm + N_EXPERTS
    flat = idx.reshape(-1)
    onehot = (flat[:, None] == jnp.arange(N_EXPERTS)[None, :]).astype(jnp.int32)
    csum = jnp.cumsum(onehot, axis=0)
    rank = jnp.sum(onehot * (csum - 1), axis=1)
    counts = csum[-1]
    padded = ((counts + tm - 1) // tm) * tm
    g_end = jnp.cumsum(padded)
    g_start = g_end - padded
    pos = jnp.sum(onehot * g_start[None, :], axis=1) + rank
    n_active = (g_end[-1] // tm).astype(jnp.int32)
    tile_start = jnp.arange(n_tiles, dtype=jnp.int32) * tm
    tile_expert = jnp.sum((tile_start[:, None] >= g_end[None, :]).astype(jnp.int32), axis=1)
    tile_expert = jnp.minimum(tile_expert, N_EXPERTS - 1)
    tile_rows = jnp.clip((g_start + counts)[tile_expert] - tile_start, 0, tm)
    is_active = jnp.arange(n_tiles) < n_active
    tile_rows = jnp.where(is_active, tile_rows, 0).astype(jnp.int32)
    last = tile_expert[n_active - 1]
    tile_expert = jnp.where(is_active, tile_expert, last).astype(jnp.int32)
    return pos.reshape(k, t).astype(jnp.int32), tile_expert, tile_rows, n_active.reshape(1)


def _dispatch_kernel(p0_ref, p1_ref, x_ref, init_hbm, o_hbm, sem, *, tt, sub):
    del init_hbm
    base = pl.program_id(0) * tt

    def body(r, carry):
        src = x_ref.at[pl.ds(pl.multiple_of(r * sub, sub), sub)]
        for p_ref in (p0_ref, p1_ref):
            dst = pl.ds(pl.multiple_of(p_ref[base + r] * sub, sub), sub)
            pltpu.make_async_copy(src, o_hbm.at[dst], sem).start()
        return carry

    lax.fori_loop(0, tt, body, 0, unroll=GATHER_UNROLL)
    for _ in range(TOP_K):
        pltpu.make_async_copy(x_ref, o_hbm.at[pl.ds(0, tt * sub)], sem).wait()


def moe_dispatch(xn, pos, n_rows, init, *, tt=2048):
    lanes = xn.shape[1]
    t = pos.shape[1]
    sub = xn.shape[0] // t
    tt = min(tt, t)
    grid_spec = pltpu.PrefetchScalarGridSpec(
        num_scalar_prefetch=2,
        grid=(t // tt,),
        in_specs=[
            pl.BlockSpec((tt * sub, lanes), lambda i, p0, p1: (i, 0)),
            pl.BlockSpec(memory_space=pl.ANY),
        ],
        out_specs=pl.BlockSpec(memory_space=pl.ANY),
        scratch_shapes=[pltpu.SemaphoreType.DMA(())],
    )
    return pl.pallas_call(
        functools.partial(_dispatch_kernel, tt=tt, sub=sub),
        grid_spec=grid_spec,
        out_shape=jax.ShapeDtypeStruct((n_rows * sub, lanes), jnp.uint32),
        input_output_aliases={3: 0},
        compiler_params=_params(("arbitrary",), 32),
        name="moe_dispatch",
    )(pos[0], pos[1], xn, jnp.zeros((n_rows * sub, lanes), jnp.uint32) if init is None else init)


def _moe_kernel(te_ref, tr_ref, na_ref, x_ref, wg0_ref, wu0_ref, wd0_ref, wg1_ref, wu1_ref, wd1_ref, o_ref,
                xb, acc, *, tm, nj):
    j = pl.program_id(1)
    n_rows = tr_ref[pl.program_id(0)]
    half = xb.shape[1] // 2
    sub = half // LANES
    n_steps = (nj + 1) // 2
    has_second = 2 * j + 1 < nj
    whole = n_rows > tm - MOE_SUB_ROWS

    def unpack_rows():
        pieces = [_unpack_pairs_f32(_load_tile_rows(x_ref, sl, tm, sub)) for sl in range(sub)]
        return jnp.concatenate([p[0].astype(BF16) for p in pieces] + [p[1].astype(BF16) for p in pieces], axis=1)

    @pl.when(jnp.logical_and(j == 0, jnp.logical_not(whole)))
    def _():
        @pl.when(n_rows > 0)
        def _():
            xb[...] = unpack_rows()

        acc[...] = jnp.zeros_like(acc)

    def weights(k):
        wg_ref, wu_ref, wd_ref = ((wg0_ref, wu0_ref, wd0_ref), (wg1_ref, wu1_ref, wd1_ref))[k]
        return wg_ref[...].astype(BF16), wu_ref[...].astype(BF16), wd_ref[...].astype(BF16)

    def swiglu_rows(rows, w, assign=False, x=None, final=False):
        wg, wu, wd = w
        x = xb[rows, :] if x is None else x
        gate = jnp.dot(x, wg, preferred_element_type=F32)
        up = jnp.dot(x, wu, preferred_element_type=F32)
        act = (_silu(gate) * up).astype(BF16)
        y = jnp.dot(act, wd, preferred_element_type=F32)
        if final:
            _store_rows_as_tiles(o_ref, _pack_bf16_pairs(y if assign else acc[rows, :] + y))
        elif assign:
            acc[rows, :] = y
        else:
            acc[rows, :] += y

    def whole_step(n_chunks, first, last):
        x = None
        if first:
            x = unpack_rows()
            xb[...] = x
        w = weights(0)
        if n_chunks == 2:
            (g0, u0, d0), (g1, u1, d1) = w, weights(1)
            w = (jnp.concatenate([g0, g1], axis=1), jnp.concatenate([u0, u1], axis=1),
                 jnp.concatenate([d0, d1], axis=0))
        swiglu_rows(slice(None), w, assign=first, x=x, final=last)

    n_last = 1 if nj % 2 else 2
    if n_steps == 1:
        pl.when(whole)(functools.partial(whole_step, n_last, True, True))
    else:
        pl.when(jnp.logical_and(whole, j == 0))(functools.partial(whole_step, 2, True, False))
        if n_steps > 2:
            middle = jnp.logical_and(j > 0, j < n_steps - 1)
            pl.when(jnp.logical_and(whole, middle))(functools.partial(whole_step, 2, False, False))
        pl.when(jnp.logical_and(whole, j == n_steps - 1))(functools.partial(whole_step, n_last, False, True))

    @pl.when(jnp.logical_and(n_rows > 0, jnp.logical_not(whole)))
    def _():
        for k in range(2):
            @pl.when(jnp.logical_or(k == 0, has_second))
            def _():
                w = weights(k)
                for sb in range(tm // MOE_SUB_ROWS):
                    @pl.when(sb * MOE_SUB_ROWS < n_rows)
                    def _():
                        swiglu_rows(slice(sb * MOE_SUB_ROWS, (sb + 1) * MOE_SUB_ROWS), w)

    @pl.when(jnp.logical_and(j == n_steps - 1, jnp.logical_not(whole)))
    def _():
        _store_rows_as_tiles(o_ref, _pack_bf16_pairs(acc[...]))


def moe_experts(xs, tile_expert, tile_rows, n_active, w_gu, w_down, layer, *, tm, tf=256):
    lanes = xs.shape[1]
    d = w_gu.shape[2]
    sub = d // 2 // lanes
    f = w_down.shape[2]
    tf = min(tf, f)
    nj = f // tf
    n_steps = (nj + 1) // 2
    n_tiles = tile_expert.shape[0]
    assert tm % MOE_SUB_ROWS == 0 and xs.shape[0] == n_tiles * tm * sub

    def chunk(k, i, j, na):
        jl = jnp.where(i < na[0], j, n_steps - 1)
        c = 2 * jl + k
        return jnp.where(c < nj, c, max(nj - 2, 0)) if k else c

    def ii(i, na):
        return jnp.minimum(i, na[0] - 1)

    def w_specs(k):
        return [
            pl.BlockSpec((None, None, d, tf), lambda i, j, te, tr, na: (layer, te[i], 0, chunk(k, i, j, na))),
            pl.BlockSpec((None, None, d, tf), lambda i, j, te, tr, na: (layer, te[i], 0, chunk(k, i, j, na) + nj)),
            pl.BlockSpec((None, None, tf, d), lambda i, j, te, tr, na: (layer, te[i], chunk(k, i, j, na), 0)),
        ]

    grid_spec = pltpu.PrefetchScalarGridSpec(
        num_scalar_prefetch=3,
        grid=(n_tiles, n_steps),
        in_specs=[pl.BlockSpec((tm * sub, lanes), lambda i, j, te, tr, na: (ii(i, na), 0))] + w_specs(0) + w_specs(1),
        out_specs=pl.BlockSpec((tm * sub, lanes), lambda i, j, te, tr, na: (i, 0)),
        scratch_shapes=[
            pltpu.VMEM((tm, d), BF16),
            pltpu.VMEM((tm, d), F32),
        ],
    )
    return pl.pallas_call(
        functools.partial(_moe_kernel, tm=tm, nj=nj),
        grid_spec=grid_spec,
        out_shape=jax.ShapeDtypeStruct((n_tiles * tm * sub, lanes), jnp.uint32),
        compiler_params=_params(("arbitrary", "arbitrary"), 60),
        name="moe_experts",
    )(tile_expert, tile_rows, n_active, xs, w_gu, w_gu, w_down, w_gu, w_gu, w_down)


def _combine_kernel(p0_ref, p1_ref, ys_hbm, h_ref, wt_ref, gf_ref, o_ref, buf, sem, *, tc, final_norm):
    i = pl.program_id(0)
    n = pl.num_programs(0)
    half = o_ref.shape[1] // 2
    sub = half // LANES

    def start_gather(tile, slot):
        base = tile * tc

        def body(r, carry):
            dst = pl.ds(pl.multiple_of(r * sub, sub), sub)
            for kk, p_ref in enumerate((p0_ref, p1_ref)):
                src = pl.ds(pl.multiple_of(p_ref[base + r] * sub, sub), sub)
                pltpu.make_async_copy(ys_hbm.at[src], buf.at[slot, kk, dst], sem.at[slot]).start()
            return carry

        lax.fori_loop(0, tc, body, 0, unroll=GATHER_UNROLL)

    @pl.when(i == 0)
    def _():
        start_gather(0, 0)

    slot = i % 2
    for kk in range(TOP_K):
        pltpu.make_async_copy(ys_hbm.at[pl.ds(0, tc * sub)], buf.at[slot, kk], sem.at[slot]).wait()

    @pl.when(i + 1 < n)
    def _():
        start_gather(i + 1, (i + 1) % 2)

    wt = wt_ref[...].T
    w0, w1 = wt[:, 0:1], wt[:, 1:2]
    for sl in range(sub):
        a_hi, a_lo = _unpack_pairs_f32(_load_tile_rows(buf.at[slot, 0], sl, tc, sub))
        b_hi, b_lo = _unpack_pairs_f32(_load_tile_rows(buf.at[slot, 1], sl, tc, sub))
        c_hi = slice(sl * LANES, (sl + 1) * LANES)
        c_lo = slice(half + sl * LANES, half + (sl + 1) * LANES)
        o_ref[:, c_hi] = h_ref[:, c_hi] + (w0 * a_hi + w1 * b_hi)
        o_ref[:, c_lo] = h_ref[:, c_lo] + (w0 * a_lo + w1 * b_lo)
    if final_norm:
        o_ref[...] = _rms(o_ref[...], gf_ref[...])


def moe_combine(ys, pos, wts, h, final_gain, *, tc=512):
    t, d = h.shape
    lanes = ys.shape[1]
    sub = d // 2 // lanes
    tc = min(tc, t)
    final_norm = final_gain is not None
    gf = (final_gain if final_norm else jnp.ones((d,), F32)).reshape(1, d)
    grid_spec = pltpu.PrefetchScalarGridSpec(
        num_scalar_prefetch=2,
        grid=(t // tc,),
        in_specs=[
            pl.BlockSpec(memory_space=pl.ANY),
            pl.BlockSpec((tc, d), lambda i, p0, p1: (i, 0)),
            pl.BlockSpec((N_EXPERTS, tc), lambda i, p0, p1: (0, i)),
            pl.BlockSpec((1, d), lambda i, p0, p1: (0, 0)),
        ],
        out_specs=pl.BlockSpec((tc, d), lambda i, p0, p1: (i, 0)),
        scratch_shapes=[
            pltpu.VMEM((2, TOP_K, tc * sub, lanes), jnp.uint32),
            pltpu.SemaphoreType.DMA((2,)),
        ],
    )
    return pl.pallas_call(
        functools.partial(_combine_kernel, tc=tc, final_norm=final_norm),
        grid_spec=grid_spec,
        out_shape=jax.ShapeDtypeStruct((t, d), F32),
        compiler_params=_params(("arbitrary",), 48),
        name="moe_combine",
    )(pos[0], pos[1], ys, h, wts, gf)


def moe_block(h, g, w_router, w_gu, w_down, layer, final_gain, xs_prev, *, tm=1024):
    t = h.shape[0]
    tm = min(tm, t)
    xn, idx, wts = router(h, g, w_router)
    pos, tile_expert, tile_rows, n_active = _routing_tables(idx[:TOP_K], tm)
    xs = moe_dispatch(xn, pos, tile_expert.shape[0] * tm, xs_prev)
    ys = moe_experts(xs, tile_expert, tile_rows, n_active, w_gu, w_down, layer, tm=tm)
    return moe_combine(ys, pos, wts, h, final_gain), xs


def kernel(x, norm_mix, norm_ffn, norm_final, ev_w_in, ev_pool_w, ev_pool_scale, ev_w_out,
           od_w_in, od_conv_w, od_w_out, ffn_w_gu, ffn_w_down, moe_router, moe_w_gu, moe_w_down):
    batch, seq, d = x.shape
    depth = norm_mix.shape[0]
    h = x.reshape(batch * seq, d)
    ev_w_in16, od_w_in16 = ev_w_in.astype(BF16), od_w_in.astype(BF16)
    ev_w_out16, od_w_out16 = ev_w_out.astype(BF16), od_w_out.astype(BF16)
    ffn_w_gu16, ffn_w_down16 = ffn_w_gu.astype(BF16), ffn_w_down.astype(BF16)
    xs = None
    for layer in range(depth):
        i = layer // 2
        if layer % 2 == 0:
            proj = norm_matmul(h, norm_mix[layer], ev_w_in16, i)
            y = even_core(proj, ev_pool_w[i].astype(BF16), ev_pool_scale[i], batch, seq)
            h = matmul_residual(y, ev_w_out16, i, h)
            h = ffn(h, norm_ffn[layer], ffn_w_gu16, ffn_w_down16, i)
        else:
            proj = norm_matmul(h, norm_mix[layer], od_w_in16, i)
            h = conv_out(proj, od_conv_w[i], od_w_out16, i, h, batch, seq)
            final_gain = norm_final if layer == depth - 1 else None
            h, xs = moe_block(h, norm_ffn[layer], moe_router[i], moe_w_gu, moe_w_down, i, final_gain, xs)
    return h.reshape(batch, seq, d)
```
